```python
import math
import jax, jax.numpy as jnp
from jax import lax
import numpy as np

D_MODEL = 1024
BATCH = 4
SEQ = 8192
DEPTH = 2

PLE_DIM = 256
HEAD_DIM = 64
N_ATTN_HEADS = 8
ATTN_WIDTH = N_ATTN_HEADS * HEAD_DIM
DILATED_BRANCHES = ((128, 1), (512, 4), (2048, 16))
ATTN_BLOCK = 128
N_REL_BUCKETS = 32
REL_MAX_DISTANCE = 2048
SGU_GROUPS = 4
SGU_GROUP_WIDTH = 64
SGU_WIDTH = SGU_GROUPS * SGU_GROUP_WIDTH
SGU_CHUNK = 128
SSM_GROUP_CH = 16
SSM_WIDTH = 256
SSM_GROUPS = SSM_WIDTH // SSM_GROUP_CH
SSM_STATE = 64
MIX_WIDTH = ATTN_WIDTH + SGU_WIDTH + SSM_WIDTH
IN_WIDTH = 3 * ATTN_WIDTH + 2 * SGU_WIDTH + SSM_WIDTH
D_FF = 2816
FFN_CONV = 3
EPS = 1e-6
NEG_INF = -1e30

kernel_name = "hybrid_dilated_sgu_s5_block"


def rms_scale(x):
    xf = x.astype(jnp.float32)
    return (xf * lax.rsqrt(jnp.mean(xf * xf, axis=-1, keepdims=True) + EPS)).astype(x.dtype)


def rms_norm(x, g):
    return rms_scale(x) * g.astype(x.dtype)


def t5_bucket(dist):
    max_exact = N_REL_BUCKETS // 2
    d = np.maximum(dist, 0)
    large = max_exact + (np.log(np.maximum(d, 1) / max_exact)
                         / np.log(REL_MAX_DISTANCE / max_exact)
                         * (N_REL_BUCKETS - max_exact)).astype(np.int32)
    large = np.minimum(large, N_REL_BUCKETS - 1)
    return np.where(d < max_exact, d, large).astype(np.int32)


def dilated_branch(q, k, v, rel_bias, window, dil):
    B, S, H, Dh = q.shape
    blk = ATTN_BLOCK
    steps = window // dil
    n = S // dil
    nblk = -(-n // blk)
    pad = nblk * blk - n

    def to_blocks(t):
        t = t.reshape(B, n, dil, H, Dh).transpose(0, 2, 1, 3, 4)
        t = jnp.pad(t, ((0, 0), (0, 0), (0, pad), (0, 0), (0, 0)))
        return t.reshape(B, dil, nblk, blk, H, Dh)

    def with_prev(t):
        prev = jnp.concatenate([jnp.zeros_like(t[:, :, :1]), t[:, :, :-1]], axis=2)
        return jnp.concatenate([prev, t], axis=3)

    qb = to_blocks(q)
    kw = with_prev(to_blocks(k))
    vw = with_prev(to_blocks(v))

    q_loc = np.arange(blk)[:, None]
    k_loc = np.arange(2 * blk)[None, :]
    rel = q_loc + blk - k_loc
    band = (rel >= 0) & (rel <= steps)
    first = (np.arange(nblk)[:, None, None] > 0) | (k_loc[None] >= blk)
    valid = (band[None] & first)[:, None]
    bias = rel_bias.astype(jnp.float32)[t5_bucket(rel * dil)]

    s = jnp.einsum('brcqhd,brckhd->brchqk', qb, kw).astype(jnp.float32) * (Dh ** -0.5)
    s = s + jnp.transpose(bias, (2, 0, 1))
    s = jnp.where(valid, s, NEG_INF)
    m = jnp.max(s, axis=-1, keepdims=True)
    e = jnp.exp(s - m)
    den = jnp.sum(e, axis=-1)
    o = jnp.einsum('brchqk,brckhd->brcqhd', e, vw.astype(jnp.float32))
    o = o / jnp.swapaxes(den, -1, -2)[..., None]
    lse = jnp.swapaxes(m[..., 0] + jnp.log(den), -1, -2)

    def from_blocks(t):
        t = t.reshape((B, dil, nblk * blk) + t.shape[4:])[:, :, :n]
        t = jnp.swapaxes(t, 1, 2)
        return t.reshape((B, S) + t.shape[3:])

    return from_blocks(o), from_blocks(lse)


def dilated_attention(q, k, v, rel_bias):
    B, S, H, Dh = q.shape
    outs, lses = [], []
    for window, dil in DILATED_BRANCHES:
        o, l = dilated_branch(q, k, v, rel_bias, window, dil)
        outs.append(o)
        lses.append(l)
    wts = jax.nn.softmax(jnp.stack(lses), axis=0)
    o = jnp.sum(wts[..., None] * jnp.stack(outs), axis=0)
    return o.reshape(B, S, H * Dh)


def spatial_gating(z, ln_g, ln_b, w_s, b_s):
    B, S, _ = z.shape
    z = jax.nn.gelu(z)
    u, v = jnp.split(z, 2, axis=-1)
    vf = v.astype(jnp.float32).reshape(B, S // SGU_CHUNK, SGU_CHUNK, SGU_GROUPS, SGU_GROUP_WIDTH)
    mu = jnp.mean(vf, axis=-1, keepdims=True)
    var = jnp.mean(jnp.square(vf - mu), axis=-1, keepdims=True)
    vn = ((vf - mu) * lax.rsqrt(var + EPS)
          * ln_g.astype(jnp.float32).reshape(SGU_GROUPS, SGU_GROUP_WIDTH)
          + ln_b.astype(jnp.float32).reshape(SGU_GROUPS, SGU_GROUP_WIDTH))
    causal = np.tril(np.ones((SGU_CHUNK, SGU_CHUNK), dtype=bool))
    w = jnp.where(causal, w_s.astype(jnp.float32), 0.0)
    mixed = jnp.einsum('gts,bnsgc->bntgc', w, vn) + b_s.astype(jnp.float32).T[:, :, None]
    return (u.astype(jnp.float32) * mixed.reshape(B, S, SGU_WIDTH)).astype(z.dtype)


def s5_ssm(u, a_re, a_im, log_dt, b_re, b_im, c_re, c_im, d_skip, glu_w, glu_b):
    B, S, _ = u.shape
    f32 = lambda t: t.astype(jnp.float32)
    lam = lax.complex(f32(a_re), f32(a_im))
    dt = jnp.exp(f32(log_dt))[:, None]
    a_bar = jnp.exp(lam * dt)
    b_bar = ((a_bar - 1.0) / lam)[:, :, None] * lax.complex(f32(b_re), f32(b_im))
    c_mat = lax.complex(f32(c_re), f32(c_im))
    ug = f32(u).reshape(B, S, SSM_GROUPS, SSM_GROUP_CH)
    bu = jnp.einsum('gnc,bsgc->bsgn', b_bar, ug.astype(jnp.complex64))
    a_seq = jnp.broadcast_to(a_bar, bu.shape)

    def combine(left, right):
        a_l, b_l = left
        a_r, b_r = right
        return a_r * a_l, a_r * b_l + b_r

    _, xs = lax.associative_scan(combine, (a_seq, bu), axis=1)
    y = jnp.real(jnp.einsum('gcn,bsgn->bsgc', c_mat, xs)) \
        + f32(d_skip).reshape(SSM_GROUPS, SSM_GROUP_CH) * ug
    y = jax.nn.gelu(y.reshape(B, S, SSM_WIDTH))
    y = y * jax.nn.sigmoid(y @ f32(glu_w) + f32(glu_b))
    return y.astype(u.dtype)


def conv_ffn(x, w_up, conv_w, conv_b, w_down):
    h = x @ w_up
    ch = h.shape[-1]
    h = lax.conv_general_dilated(h, conv_w[:, None, :].astype(h.dtype), window_strides=(1,),
                                 padding=((FFN_CONV - 1, 0),),
                                 dimension_numbers=('NWC', 'WIO', 'NWC'),
                                 feature_group_count=ch) + conv_b
    val, gate = jnp.split(h, 2, axis=-1)
    return (jax.nn.gelu(gate) * val) @ w_down


def setup_inputs(seed: int = 0) -> dict:
    key = jax.random.key(seed)
    ks = jax.random.split(key, 30)
    nrm = lambda k, shape, scale: jax.random.normal(k, shape, jnp.float32) * scale
    gain = lambda k, shape: 1.0 + 0.05 * jax.random.normal(k, shape, jnp.float32)
    L = DEPTH
    return {
        "x": nrm(ks[0], (BATCH, SEQ, D_MODEL), 1.0),
        "p": nrm(ks[1], (DEPTH, BATCH, SEQ, PLE_DIM), 1.0),
        "rel_bias": nrm(ks[2], (N_REL_BUCKETS, N_ATTN_HEADS), 0.5),
        "norm_attn_g": gain(ks[3], (L, D_MODEL)),
        "w_in": nrm(ks[4], (L, D_MODEL, IN_WIDTH), D_MODEL ** -0.5),
        "sgu_ln_g": gain(ks[5], (L, SGU_WIDTH)),
        "sgu_ln_b": nrm(ks[6], (L, SGU_WIDTH), 0.02),
        "sgu_w": nrm(ks[7], (L, SGU_GROUPS, SGU_CHUNK, SGU_CHUNK), 0.5 * SGU_CHUNK ** -0.5),
        "sgu_b": 1.0 + nrm(ks[8], (L, SGU_GROUPS, SGU_CHUNK), 0.01),
        "ssm_a_re": -0.5 + nrm(ks[9], (L, SSM_GROUPS, SSM_STATE), 0.01),
        "ssm_a_im": math.pi * jnp.arange(SSM_STATE, dtype=jnp.float32)
                    + nrm(ks[10], (L, SSM_GROUPS, SSM_STATE), 0.01),
        "ssm_log_dt": jax.random.uniform(ks[11], (L, SSM_GROUPS), jnp.float32,
                                         math.log(1e-3), math.log(1e-1)),
        "ssm_b_re": nrm(ks[12], (L, SSM_GROUPS, SSM_STATE, SSM_GROUP_CH), (2 * SSM_GROUP_CH) ** -0.5),
        "ssm_b_im": nrm(ks[13], (L, SSM_GROUPS, SSM_STATE, SSM_GROUP_CH), (2 * SSM_GROUP_CH) ** -0.5),
        "ssm_c_re": nrm(ks[14], (L, SSM_GROUPS, SSM_GROUP_CH, SSM_STATE), SSM_STATE ** -0.5),
        "ssm_c_im": nrm(ks[15], (L, SSM_GROUPS, SSM_GROUP_CH, SSM_STATE), SSM_STATE ** -0.5),
        "ssm_d": nrm(ks[16], (L, SSM_WIDTH), 1.0),
        "ssm_glu_w": nrm(ks[17], (L, SSM_WIDTH, SSM_WIDTH), SSM_WIDTH ** -0.5),
        "ssm_glu_b": nrm(ks[18], (L, SSM_WIDTH), 0.01),
        "branch_norm_g": gain(ks[19], (L, MIX_WIDTH)),
        "w_out": nrm(ks[20], (L, MIX_WIDTH, D_MODEL), MIX_WIDTH ** -0.5),
        "norm_ffn_g": gain(ks[21], (L, D_MODEL)),
        "ffn_w_up": nrm(ks[22], (L, D_MODEL, 2 * D_FF), D_MODEL ** -0.5),
        "ffn_conv_w": nrm(ks[23], (L, FFN_CONV, 2 * D_FF), FFN_CONV ** -0.5),
        "ffn_conv_b": nrm(ks[24], (L, 2 * D_FF), 0.01),
        "ffn_w_down": nrm(ks[25], (L, D_FF, D_MODEL), D_FF ** -0.5),
        "norm_ple_g": gain(ks[26], (L, D_MODEL)),
        "ple_w_gate": nrm(ks[27], (L, D_MODEL, D_MODEL), D_MODEL ** -0.5),
        "ple_w_proj": nrm(ks[28], (L, PLE_DIM, D_MODEL), PLE_DIM ** -0.5),
        "final_norm_g": gain(ks[29], (D_MODEL,)),
    }


def reference(x, p, rel_bias, norm_attn_g, w_in, sgu_ln_g, sgu_ln_b, sgu_w, sgu_b,
              ssm_a_re, ssm_a_im, ssm_log_dt, ssm_b_re, ssm_b_im, ssm_c_re, ssm_c_im,
              ssm_d, ssm_glu_w, ssm_glu_b, branch_norm_g, w_out, norm_ffn_g,
              ffn_w_up, ffn_conv_w, ffn_conv_b, ffn_w_down, norm_ple_g, ple_w_gate,
              ple_w_proj, final_norm_g):
    B, S, _ = x.shape
    o_k = ATTN_WIDTH
    o_v = 2 * ATTN_WIDTH
    o_g = 3 * ATTN_WIDTH
    o_s = o_g + 2 * SGU_WIDTH
    h = x
    for i in range(DEPTH):
        z = rms_norm(h, norm_attn_g[i]) @ w_in[i]
        q = z[..., :o_k].reshape(B, S, N_ATTN_HEADS, HEAD_DIM)
        k = z[..., o_k:o_v].reshape(B, S, N_ATTN_HEADS, HEAD_DIM)
        v = z[..., o_v:o_g].reshape(B, S, N_ATTN_HEADS, HEAD_DIM)
        y_attn = dilated_attention(q, k, v, rel_bias).astype(h.dtype)
        y_sgu = spatial_gating(z[..., o_g:o_s], sgu_ln_g[i], sgu_ln_b[i], sgu_w[i], sgu_b[i])
        y_ssm = s5_ssm(z[..., o_s:], ssm_a_re[i], ssm_a_im[i], ssm_log_dt[i],
                       ssm_b_re[i], ssm_b_im[i], ssm_c_re[i], ssm_c_im[i],
                       ssm_d[i], ssm_glu_w[i], ssm_glu_b[i])
        mix = jnp.concatenate([rms_scale(y_attn), rms_scale(y_sgu), rms_scale(y_ssm)],
                              axis=-1) * branch_norm_g[i]
        h = h + mix @ w_out[i]
        h = h + conv_ffn(rms_norm(h, norm_ffn_g[i]), ffn_w_up[i], ffn_conv_w[i],
                         ffn_conv_b[i], ffn_w_down[i])
        gate = jax.nn.sigmoid(rms_norm(h, norm_ple_g[i]) @ ple_w_gate[i])
        h = h + gate * (p[i] @ ple_w_proj[i])
    return rms_norm(h, final_norm_g)
```

```python
import functools
import math

import numpy as np
import jax
import jax.numpy as jnp
from jax import lax
from jax.experimental import pallas as pl
from jax.experimental.pallas import tpu as pltpu

D_MODEL = 1024
DEPTH = 2
PLE_DIM = 256
HEAD_DIM = 64
N_HEADS = 8
ATTN_WIDTH = N_HEADS * HEAD_DIM
BRANCHES = ((128, 1), (512, 4), (2048, 16))
ATTN_BLOCK = 128
N_REL_BUCKETS = 32
REL_MAX_DISTANCE = 2048
SGU_GROUPS = 4
SGU_GROUP_WIDTH = 64
SGU_WIDTH = SGU_GROUPS * SGU_GROUP_WIDTH
SGU_CHUNK = 128
SSM_GROUP_CH = 16
SSM_WIDTH = 256
SSM_GROUPS = SSM_WIDTH // SSM_GROUP_CH
SSM_STATE = 64
SSM_LANES = SSM_GROUPS * SSM_STATE
QKV_WIDTH = 3 * ATTN_WIDTH
IN_WIDTH = QKV_WIDTH + 2 * SGU_WIDTH + SSM_WIDTH
D_FF = 2816
EPS = 1e-6
NEG_INF = -1e30

LANES = 128
SUBLANES = 8
VMEM_LIMIT = 56 * 1024 * 1024

ROW_TILE = 512
SSM_SEG = 64
SSM_TILE = SUBLANES * SSM_SEG
FF_CHUNK = 256

BF16 = jnp.bfloat16
F32 = jnp.float32


def _gelu(x):
    return 0.5 * x * (1.0 + jnp.tanh(0.7978845608028654 * (x + 0.044715 * (x * x * x))))


def _sigmoid(x):
    return 1.0 / (1.0 + jnp.exp(-x))


def _rms_scale(x):
    return x * lax.rsqrt(jnp.mean(x * x, axis=-1, keepdims=True) + EPS)


def _dot(a, b):
    return jnp.dot(a, b, preferred_element_type=F32)


def _params(*sem):
    return pltpu.CompilerParams(dimension_semantics=sem, vmem_limit_bytes=VMEM_LIMIT)


def _resident(shape):
    nd = len(shape)
    return pl.BlockSpec(shape, lambda *_: (0,) * nd, pipeline_mode=pl.Buffered(1))


def _inproj_kernel(x_ref, g_ref, w_ref, qkv_ref, sgu_ref, ssm_ref):
    xn = (_rms_scale(x_ref[0]) * g_ref[...]).astype(BF16)
    for c in range(0, QKV_WIDTH, 512):
        qkv_ref[0, :, c:c + 512] = _dot(xn, w_ref[:, c:c + 512]).astype(BF16)
    for c in range(0, 2 * SGU_WIDTH, 256):
        sgu_ref[0, :, c:c + 256] = _dot(xn, w_ref[:, QKV_WIDTH + c:QKV_WIDTH + c + 256])
    ssm_ref[0] = _dot(xn, w_ref[:, QKV_WIDTH + 2 * SGU_WIDTH:])


def _inproj(h, g, w):
    B, S, _ = h.shape
    return pl.pallas_call(
        _inproj_kernel,
        grid=(B, S // ROW_TILE),
        in_specs=[pl.BlockSpec((1, ROW_TILE, D_MODEL), lambda b, s: (b, s, 0)),
                  _resident((1, D_MODEL)), _resident((D_MODEL, IN_WIDTH))],
        out_specs=[pl.BlockSpec((1, ROW_TILE, QKV_WIDTH), lambda b, s: (b, s, 0)),
                   pl.BlockSpec((1, ROW_TILE, 2 * SGU_WIDTH), lambda b, s: (b, s, 0)),
                   pl.BlockSpec((1, ROW_TILE, SSM_WIDTH), lambda b, s: (b, s, 0))],
        out_shape=[jax.ShapeDtypeStruct((B, S, QKV_WIDTH), BF16),
                   jax.ShapeDtypeStruct((B, S, 2 * SGU_WIDTH), F32),
                   jax.ShapeDtypeStruct((B, S, SSM_WIDTH), F32)],
        compiler_params=_params("parallel", "parallel"),
        name="inproj",
    )(h, g, w)


def _t5_bucket(dist):
    max_exact = N_REL_BUCKETS // 2
    d = np.maximum(dist, 0)
    large = max_exact + (np.log(np.maximum(d, 1) / max_exact)
                         / np.log(REL_MAX_DISTANCE / max_exact)
                         * (N_REL_BUCKETS - max_exact)).astype(np.int32)
    large = np.minimum(large, N_REL_BUCKETS - 1)
    return np.where(d < max_exact, d, large).astype(np.int32)


def _bias_table(rel_bias, window, dil):
    blk = ATTN_BLOCK
    rel = np.arange(blk)[:, None] + blk - np.arange(2 * blk)[None, :]
    band = (rel >= 0) & (rel <= window // dil)
    bias = rel_bias.astype(F32)[_t5_bucket(rel * dil)]
    return jnp.where(band[None], jnp.transpose(bias, (2, 0, 1)), NEG_INF)


def _attn_kernel(q_ref, kc_ref, kp_ref, vc_ref, vp_ref, bias_ref, o_ref, lse_ref,
                 k_scr, v_scr, *, q_blocks):
    blk = ATTN_BLOCK
    first_tile = pl.program_id(2) == 0
    k_scr[0:blk, :] = kp_ref[0]
    k_scr[blk:, :] = kc_ref[0]
    v_scr[0:blk, :] = vp_ref[0]
    v_scr[blk:, :] = vc_ref[0]
    lane = lax.broadcasted_iota(jnp.int32, (blk, LANES), 1)
    low_half = lane < HEAD_DIM
    col = lax.broadcasted_iota(jnp.int32, (blk, 2 * blk), 1)
    ones = jnp.ones((2 * blk, LANES), BF16)

    def q_block(qb, carry):
        row0 = pl.multiple_of(qb * blk, blk)
        no_prev = jnp.logical_and(first_tile, qb == 0)
        prev_mask = jnp.where(jnp.logical_and(no_prev, col < blk), NEG_INF, 0.0)
        for hp in range(N_HEADS // 2):
            cols = slice(hp * LANES, (hp + 1) * LANES)
            q = q_ref[0, pl.ds(row0, blk), cols]
            kk = k_scr[pl.ds(row0, 2 * blk), cols]
            vv = jnp.concatenate([v_scr[pl.ds(row0, 2 * blk), cols], ones], axis=1)
            outs, lses = [], []
            for hh in range(2):
                qm = jnp.where(low_half if hh == 0 else jnp.logical_not(low_half), q, jnp.zeros_like(q))
                s = lax.dot_general(qm, kk, (((1,), (1,)), ((), ())), preferred_element_type=F32)
                s = s * (HEAD_DIM ** -0.5) + bias_ref[2 * hp + hh]
                s = s + prev_mask
                m = jnp.max(s, axis=-1, keepdims=True)
                e = jnp.exp(s - m)
                oe = _dot(e.astype(BF16), vv)
                den = oe[:, LANES:]
                outs.append(oe[:, :LANES] / den)
                lses.append(m + jnp.log(den))
            o_ref[0, pl.ds(row0, blk), cols] = jnp.where(low_half, outs[0], outs[1]).astype(BF16)
            lse_ref[0, pl.ds(row0, blk), cols] = jnp.where(low_half, lses[0], lses[1])
        return carry

    lax.fori_loop(0, q_blocks, q_block, 0)


def _attn_branch(qkv, bias, dil):
    B, S, _ = qkv.shape
    n = S // dil
    tq = min(n, 512)
    q_blocks = tq // ATTN_BLOCK
    view = qkv.reshape(B, n, dil * QKV_WIDTH)
    W = ATTN_WIDTH
    cur = lambda part: pl.BlockSpec((1, tq, W), lambda b, r, j: (b, j, 3 * r + part))
    prev = lambda part: pl.BlockSpec(
        (1, ATTN_BLOCK, W), lambda b, r, j: (b, jnp.maximum(j * q_blocks - 1, 0), 3 * r + part))
    out_spec = pl.BlockSpec((1, tq, W), lambda b, r, j: (b, j, r))
    o, lse = pl.pallas_call(
        functools.partial(_attn_kernel, q_blocks=q_blocks),
        grid=(B, dil, n // tq),
        in_specs=[cur(0), cur(1), prev(1), cur(2), prev(2), _resident((N_HEADS, ATTN_BLOCK, 2 * ATTN_BLOCK))],
        out_specs=[out_spec, out_spec],
        out_shape=[jax.ShapeDtypeStruct((B, n, dil * W), BF16), jax.ShapeDtypeStruct((B, n, dil * W), F32)],
        scratch_shapes=[pltpu.VMEM((tq + ATTN_BLOCK, W), BF16), pltpu.VMEM((tq + ATTN_BLOCK, W), BF16)],
        compiler_params=_params("parallel", "parallel", "parallel"),
        name=f"attn_d{dil}",
    )(view, view, view, view, view, bias)
    return o.reshape(B, S, W), lse.reshape(B, S, W)


def _sgu_kernel(z_ref, avg_ref, lng_ref, lnb_ref, w_ref, bt_ref, y_ref):
    z = _gelu(z_ref[0])
    u = z[:, :SGU_WIDTH]
    v = z[:, SGU_WIDTH:]
    avg = avg_ref[...]

    def group_mean(t):
        hi = t.astype(BF16)
        lo = (t - hi.astype(F32)).astype(BF16)
        return _dot(hi, avg) + _dot(lo, avg)

    d = v - group_mean(v)
    vn = d * lax.rsqrt(group_mean(d * d) + EPS) * lng_ref[...] + lnb_ref[...]
    lane = lax.broadcasted_iota(jnp.int32, (SGU_CHUNK, LANES), 1)
    low_half = lane < SGU_GROUP_WIDTH
    for c in range(ROW_TILE // SGU_CHUNK):
        rows = slice(c * SGU_CHUNK, (c + 1) * SGU_CHUNK)
        for p in range(SGU_GROUPS // 2):
            cols = slice(p * LANES, (p + 1) * LANES)
            vp = vn[rows, cols].astype(BF16)
            mixed = jnp.where(low_half, _dot(w_ref[2 * p], vp), _dot(w_ref[2 * p + 1], vp))
            y_ref[0, rows, cols] = u[rows, cols] * (mixed + bt_ref[:, cols])


def _sgu(z, avg, ln_g, ln_b, w_causal, b_table):
    B, S, _ = z.shape
    return pl.pallas_call(
        _sgu_kernel,
        grid=(B, S // ROW_TILE),
        in_specs=[pl.BlockSpec((1, ROW_TILE, 2 * SGU_WIDTH), lambda b, s: (b, s, 0)),
                  _resident((SGU_WIDTH, SGU_WIDTH)), _resident((1, SGU_WIDTH)), _resident((1, SGU_WIDTH)),
                  _resident((SGU_GROUPS, SGU_CHUNK, SGU_CHUNK)), _resident((SGU_CHUNK, SGU_WIDTH))],
        out_specs=pl.BlockSpec((1, ROW_TILE, SGU_WIDTH), lambda b, s: (b, s, 0)),
        out_shape=jax.ShapeDtypeStruct((B, S, SGU_WIDTH), F32),
        compiler_params=_params("parallel", "parallel"),
        name="sgu",
    )(z, avg, ln_g, ln_b, w_causal, b_table)


def _ssm_kernel(u_ref, bmat_ref, cmat_ref, a_ref, aseg_ref, d_ref, gw_ref, gb_ref, y_ref,
                x_scr, carry_scr, xin_scr):
    N = SSM_LANES

    @pl.when(pl.program_id(1) == 0)
    def _():
        carry_scr[...] = jnp.zeros_like(carry_scr)

    ublk = u_ref[0]
    u = jnp.concatenate([ublk[:, t * SSM_WIDTH:(t + 1) * SSM_WIDTH] for t in range(SSM_SEG)], axis=0)
    x_scr[...] = _dot(u.astype(BF16), bmat_ref[...])

    a_re = jnp.broadcast_to(a_ref[0:1, :], (SUBLANES, N))
    a_im = jnp.broadcast_to(a_ref[1:2, :], (SUBLANES, N))

    def local_step(t, x):
        xr, xi = x
        rows = pl.ds(pl.multiple_of(t * SUBLANES, SUBLANES), SUBLANES)
        nr = a_re * xr - a_im * xi + x_scr[rows, 0:N]
        ni = a_re * xi + a_im * xr + x_scr[rows, N:2 * N]
        x_scr[rows, 0:N] = nr
        x_scr[rows, N:2 * N] = ni
        return nr, ni

    zeros = jnp.zeros((SUBLANES, N), F32)
    end_re, end_im = lax.fori_loop(0, SSM_SEG, local_step, (zeros, zeros), unroll=2)

    s_re, s_im = aseg_ref[0:1, :], aseg_ref[1:2, :]
    cr, ci = carry_scr[0:1, :], carry_scr[1:2, :]
    for j in range(SUBLANES):
        xin_scr[j:j + 1, 0:N] = cr
        xin_scr[j:j + 1, N:2 * N] = ci
        er, ei = end_re[j:j + 1, :], end_im[j:j + 1, :]
        cr, ci = s_re * cr - s_im * ci + er, s_re * ci + s_im * cr + ei
    carry_scr[0:1, :] = cr
    carry_scr[1:2, :] = ci

    def fix_step(t, f):
        fr, fi = f
        rows = pl.ds(pl.multiple_of(t * SUBLANES, SUBLANES), SUBLANES)
        nr = a_re * fr - a_im * fi
        ni = a_re * fi + a_im * fr
        x_scr[rows, 0:N] = x_scr[rows, 0:N] + nr
        x_scr[rows, N:2 * N] = x_scr[rows, N:2 * N] + ni
        return nr, ni

    lax.fori_loop(0, SSM_SEG, fix_step, (xin_scr[:, 0:N], xin_scr[:, N:2 * N]), unroll=2)

    y = _dot(x_scr[...].astype(BF16), cmat_ref[...]) + d_ref[...] * u
    y = _gelu(y)
    y = y * _sigmoid(_dot(y.astype(BF16), gw_ref[...]) + gb_ref[...])
    for t in range(SSM_SEG):
        y_ref[0, :, t * SSM_WIDTH:(t + 1) * SSM_WIDTH] = y[t * SUBLANES:(t + 1) * SUBLANES, :]


def _ssm(u, bmat, cmat, a_step, a_seg, d_skip, glu_w, glu_b):
    B, S, _ = u.shape
    view = u.reshape(B, S // SSM_SEG, SSM_SEG * SSM_WIDTH)
    blk = pl.BlockSpec((1, SUBLANES, SSM_SEG * SSM_WIDTH), lambda b, s: (b, s, 0))
    y = pl.pallas_call(
        _ssm_kernel,
        grid=(B, S // SSM_TILE),
        in_specs=[blk, _resident((SSM_WIDTH, 2 * SSM_LANES)), _resident((2 * SSM_LANES, SSM_WIDTH)),
                  _resident((2, SSM_LANES)), _resident((2, SSM_LANES)), _resident((1, SSM_WIDTH)),
                  _resident((SSM_WIDTH, SSM_WIDTH)), _resident((1, SSM_WIDTH))],
        out_specs=blk,
        out_shape=jax.ShapeDtypeStruct(view.shape, F32),
        scratch_shapes=[pltpu.VMEM((SSM_TILE, 2 * SSM_LANES), F32), pltpu.VMEM((2, SSM_LANES), F32),
                        pltpu.VMEM((SUBLANES, 2 * SSM_LANES), F32)],
        compiler_params=_params("parallel", "arbitrary"),
        name="ssm",
    )(view, bmat, cmat, a_step, a_seg, d_skip, glu_w, glu_b)
    return y.reshape(B, S, SSM_WIDTH)


def _ssm_matrices(a_re, a_im, log_dt, b_re, b_im, c_re, c_im):
    G, N, C = SSM_GROUPS, SSM_STATE, SSM_GROUP_CH
    dt = jnp.exp(log_dt)[:, None]
    mag = jnp.exp(a_re * dt)
    ab_re, ab_im = mag * jnp.cos(a_im * dt), mag * jnp.sin(a_im * dt)
    den = a_re * a_re + a_im * a_im
    f_re = ((ab_re - 1.0) * a_re + ab_im * a_im) / den
    f_im = (ab_im * a_re - (ab_re - 1.0) * a_im) / den
    bb_re = f_re[:, :, None] * b_re - f_im[:, :, None] * b_im
    bb_im = f_re[:, :, None] * b_im + f_im[:, :, None] * b_re
    eye = jnp.eye(G, dtype=F32)
    bd_in = lambda t: jnp.einsum('gnc,gh->gchn', t, eye).reshape(G * C, G * N)
    bd_out = lambda t: jnp.einsum('gcn,gh->gnhc', t, eye).reshape(G * N, G * C)
    bmat = jnp.concatenate([bd_in(bb_re), bd_in(bb_im)], axis=1).astype(BF16)
    cmat = jnp.concatenate([bd_out(c_re), -bd_out(c_im)], axis=0).astype(BF16)
    a_step = jnp.stack([ab_re.reshape(-1), ab_im.reshape(-1)])
    pr, pi = ab_re, ab_im
    for _ in range(int(math.log2(SSM_SEG))):
        pr, pi = pr * pr - pi * pi, 2.0 * pr * pi
    a_seg = jnp.stack([pr.reshape(-1), pi.reshape(-1)])
    return bmat, cmat, a_step, a_seg


def _outproj_kernel(o1, o2, o3, l1, l2, l3, sgu_ref, ssm_ref, h_ref, g_ref, w_ref, out_ref):
    ls = [l1[0], l2[0], l3[0]]
    m = jnp.maximum(jnp.maximum(ls[0], ls[1]), ls[2])
    es = [jnp.exp(l - m) for l in ls]
    os_ = [o1[0].astype(F32), o2[0].astype(F32), o3[0].astype(F32)]
    attn = (es[0] * os_[0] + es[1] * os_[1] + es[2] * os_[2]) / (es[0] + es[1] + es[2])
    acc = h_ref[0]
    lo = 0
    for part in (attn, sgu_ref[0], ssm_ref[0]):
        hi = lo + part.shape[-1]
        acc = acc + _dot((_rms_scale(part) * g_ref[:, lo:hi]).astype(BF16), w_ref[lo:hi, :])
        lo = hi
    out_ref[0] = acc


def _outproj(os_, ls, y_sgu, y_ssm, h, g, w):
    B, S, _ = h.shape
    tile = lambda width: pl.BlockSpec((1, ROW_TILE, width), lambda b, s: (b, s, 0))
    return pl.pallas_call(
        _outproj_kernel,
        grid=(B, S // ROW_TILE),
        in_specs=[tile(ATTN_WIDTH)] * 6 + [tile(SGU_WIDTH), tile(SSM_WIDTH), tile(D_MODEL),
                                           _resident((1, D_MODEL)), _resident((D_MODEL, D_MODEL))],
        out_specs=tile(D_MODEL),
        out_shape=jax.ShapeDtypeStruct(h.shape, F32),
        compiler_params=_params("parallel", "parallel"),
        name="outproj",
    )(*os_, *ls, y_sgu, y_ssm, h, g, w)


def _ffn_kernel(h_ref, p_ref, gffn_ref, wup_ref, cw_ref, cb_ref, wdown_ref, gple_ref, wgate_ref, wproj_ref,
                gfin_ref, out_ref, xn_scr, tail_scr, *, final):
    n_chunks = D_FF // FF_CHUNK
    first_tile = pl.program_id(1) == 0
    h = h_ref[0]
    xn_scr[...] = (_rms_scale(h) * gffn_ref[...]).astype(BF16)
    out_ref[0] = h

    def conv(t, col0, slot):
        cols = slice(col0, col0 + FF_CHUNK)
        tail = jnp.where(first_tile, 0.0, tail_scr[slot])
        ext = jnp.concatenate([tail, t], axis=0)
        m1 = pltpu.roll(ext, 1, axis=0)[SUBLANES:]
        m2 = pltpu.roll(ext, 2, axis=0)[SUBLANES:]
        tail_scr[slot] = t[ROW_TILE - SUBLANES:]
        return cw_ref[2:3, cols] * t + cw_ref[1:2, cols] * m1 + cw_ref[0:1, cols] * m2 + cb_ref[:, cols]

    for c in range(n_chunks):
        v0, g0 = c * FF_CHUNK, D_FF + c * FF_CHUNK
        xn = xn_scr[...]
        val = conv(_dot(xn, wup_ref[:, v0:v0 + FF_CHUNK]), v0, 2 * c)
        gate = conv(_dot(xn, wup_ref[:, g0:g0 + FF_CHUNK]), g0, 2 * c + 1)
        act = (_gelu(gate) * val).astype(BF16)
        out_ref[0] += _dot(act, wdown_ref[c * FF_CHUNK:(c + 1) * FF_CHUNK, :])

    h2 = out_ref[0]
    gate = _sigmoid(_dot((_rms_scale(h2) * gple_ref[...]).astype(BF16), wgate_ref[...]))
    h3 = h2 + gate * _dot(p_ref[0].astype(BF16), wproj_ref[...])
    if final:
        h3 = _rms_scale(h3) * gfin_ref[...]
    out_ref[0] = h3


def _ffn(h, p, g_ffn, w_up, conv_w, conv_b, w_down, g_ple, w_gate, w_proj, g_final, final):
    B, S, _ = h.shape
    tile = lambda width: pl.BlockSpec((1, ROW_TILE, width), lambda b, s: (b, s, 0))
    return pl.pallas_call(
        functools.partial(_ffn_kernel, final=final),
        grid=(B, S // ROW_TILE),
        in_specs=[tile(D_MODEL), tile(PLE_DIM), _resident((1, D_MODEL)), _resident((D_MODEL, 2 * D_FF)),
                  _resident((3, 2 * D_FF)), _resident((1, 2 * D_FF)), _resident((D_FF, D_MODEL)),
                  _resident((1, D_MODEL)), _resident((D_MODEL, D_MODEL)), _resident((PLE_DIM, D_MODEL)),
                  _resident((1, D_MODEL))],
        out_specs=tile(D_MODEL),
        out_shape=jax.ShapeDtypeStruct(h.shape, F32),
        scratch_shapes=[pltpu.VMEM((ROW_TILE, D_MODEL), BF16),
                        pltpu.VMEM((2 * (D_FF // FF_CHUNK), SUBLANES, FF_CHUNK), F32)],
        compiler_params=_params("parallel", "arbitrary"),
        name="ffn",
    )(h, p, g_ffn, w_up, conv_w, conv_b, w_down, g_ple, w_gate, w_proj, g_final)


def kernel(x, p, rel_bias, norm_attn_g, w_in, sgu_ln_g, sgu_ln_b, sgu_w, sgu_b, ssm_a_re, ssm_a_im, ssm_log_dt, ssm_b_re, ssm_b_im, ssm_c_re, ssm_c_im, ssm_d, ssm_glu_w, ssm_glu_b, branch_norm_g, w_out, norm_ffn_g, ffn_w_up, ffn_conv_w, ffn_conv_b, ffn_w_down, norm_ple_g, ple_w_gate, ple_w_proj, final_norm_g):
    row = lambda t: t.reshape(1, -1).astype(F32)
    biases = [_bias_table(rel_bias, window, dil) for window, dil in BRANCHES]
    group_avg = jnp.asarray(np.kron(np.eye(SGU_GROUPS), np.full((SGU_GROUP_WIDTH,) * 2, 1.0 / SGU_GROUP_WIDTH)), BF16)
    causal = np.tril(np.ones((SGU_CHUNK, SGU_CHUNK), dtype=bool))

    h = x
    for i in range(DEPTH):
        qkv, z_sgu, z_ssm = _inproj(h, row(norm_attn_g[i]), w_in[i].astype(BF16))

        branch = [_attn_branch(qkv, bias, dil) for bias, (_, dil) in zip(biases, BRANCHES)]

        w_causal = jnp.where(causal, sgu_w[i].astype(F32), 0.0).astype(BF16)
        b_table = jnp.repeat(sgu_b[i].astype(F32).T, SGU_GROUP_WIDTH, axis=1)
        y_sgu = _sgu(z_sgu, group_avg, row(sgu_ln_g[i]), row(sgu_ln_b[i]), w_causal, b_table)

        bmat, cmat, a_step, a_seg = _ssm_matrices(ssm_a_re[i], ssm_a_im[i], ssm_log_dt[i], ssm_b_re[i],
                                                  ssm_b_im[i], ssm_c_re[i], ssm_c_im[i])
        y_ssm = _ssm(z_ssm, bmat, cmat, a_step, a_seg, row(ssm_d[i]), ssm_glu_w[i].astype(BF16),
                     row(ssm_glu_b[i]))

        h = _outproj([o for o, _ in branch], [l for _, l in branch], y_sgu, y_ssm, h,
                     row(branch_norm_g[i]), w_out[i].astype(BF16))

        h = _ffn(h, p[i], row(norm_ffn_g[i]), ffn_w_up[i].astype(BF16), ffn_conv_w[i].astype(F32),
                 row(ffn_conv_b[i]), ffn_w_down[i].astype(BF16), row(norm_ple_g[i]),
                 ple_w_gate[i].astype(BF16), ple_w_proj[i].astype(BF16), row(final_norm_g),
                 final=(i == DEPTH - 1))
    return h
```

```python
import functools
import math

import numpy as np
import jax
import jax.numpy as jnp
from jax import lax
from jax.experimental import pallas as pl
from jax.experimental.pallas import tpu as pltpu

D_MODEL = 1024
DEPTH = 2
PLE_DIM = 256
HEAD_DIM = 64
N_HEADS = 8
ATTN_WIDTH = N_HEADS * HEAD_DIM
BRANCHES = ((128, 1), (512, 4), (2048, 16))
ATTN_BLOCK = 128
N_REL_BUCKETS = 32
REL_MAX_DISTANCE = 2048
SGU_GROUPS = 4
SGU_GROUP_WIDTH = 64
SGU_WIDTH = SGU_GROUPS * SGU_GROUP_WIDTH
SGU_CHUNK = 128
SSM_GROUP_CH = 16
SSM_WIDTH = 256
SSM_GROUPS = SSM_WIDTH // SSM_GROUP_CH
SSM_STATE = 64
SSM_LANES = SSM_GROUPS * SSM_STATE
QKV_WIDTH = 3 * ATTN_WIDTH
IN_WIDTH = QKV_WIDTH + 2 * SGU_WIDTH + SSM_WIDTH
D_FF = 2816
EPS = 1e-6
NEG_INF = -1e30

LANES = 128
SUBLANES = 8
VMEM_LIMIT = 56 * 1024 * 1024

ROW_TILE = 512
ATTN_SUPER = ATTN_BLOCK * max(d for _, d in BRANCHES)
SSM_SEG = 64
SSM_TILE = SUBLANES * SSM_SEG
SSM_PITCH = SSM_SEG + SUBLANES
FF_CHUNK = 256

BF16 = jnp.bfloat16
F32 = jnp.float32


def _gelu(x):
    return 0.5 * x * (1.0 + jnp.tanh(0.7978845608028654 * (x + 0.044715 * (x * x * x))))


def _sigmoid(x):
    return 1.0 / (1.0 + jnp.exp(-x))


def _rms_scale(x):
    return x * lax.rsqrt(jnp.mean(x * x, axis=-1, keepdims=True) + EPS)


def _dot(a, b):
    return jnp.dot(a, b, preferred_element_type=F32)


def _params(*sem):
    return pltpu.CompilerParams(dimension_semantics=sem, vmem_limit_bytes=VMEM_LIMIT)


def _resident(shape):
    nd = len(shape)
    return pl.BlockSpec(shape, lambda *_: (0,) * nd, pipeline_mode=pl.Buffered(1))


def _rows(start, size, stride):
    return pl.ds(start, size) if stride == 1 else pl.ds(start, size, stride=stride)


def _inproj_kernel(x_ref, g_ref, w_ref, qkv_ref, sgu_ref, ssm_ref):
    xn = (_rms_scale(x_ref[0]) * g_ref[...]).astype(BF16)
    for c in range(0, QKV_WIDTH, 512):
        qkv_ref[0, :, c:c + 512] = _dot(xn, w_ref[:, c:c + 512])
    for c in range(0, 2 * SGU_WIDTH, 256):
        sgu_ref[0, :, c:c + 256] = _dot(xn, w_ref[:, QKV_WIDTH + c:QKV_WIDTH + c + 256])
    ssm_ref[0] = _dot(xn, w_ref[:, QKV_WIDTH + 2 * SGU_WIDTH:])


def _inproj(h, g, w):
    B, S, _ = h.shape
    return pl.pallas_call(
        _inproj_kernel,
        grid=(B, S // ROW_TILE),
        in_specs=[pl.BlockSpec((1, ROW_TILE, D_MODEL), lambda b, s: (b, s, 0)),
                  _resident((1, D_MODEL)), _resident((D_MODEL, IN_WIDTH))],
        out_specs=[pl.BlockSpec((1, ROW_TILE, QKV_WIDTH), lambda b, s: (b, s, 0)),
                   pl.BlockSpec((1, ROW_TILE, 2 * SGU_WIDTH), lambda b, s: (b, s, 0)),
                   pl.BlockSpec((1, ROW_TILE, SSM_WIDTH), lambda b, s: (b, s, 0))],
        out_shape=[jax.ShapeDtypeStruct((B, S, QKV_WIDTH), F32),
                   jax.ShapeDtypeStruct((B, S, 2 * SGU_WIDTH), F32),
                   jax.ShapeDtypeStruct((B, S, SSM_WIDTH), F32)],
        compiler_params=_params("parallel", "parallel"),
        name="inproj",
    )(h, g, w)


def _t5_bucket(dist):
    max_exact = N_REL_BUCKETS // 2
    d = np.maximum(dist, 0)
    large = max_exact + (np.log(np.maximum(d, 1) / max_exact)
                         / np.log(REL_MAX_DISTANCE / max_exact)
                         * (N_REL_BUCKETS - max_exact)).astype(np.int32)
    large = np.minimum(large, N_REL_BUCKETS - 1)
    return np.where(d < max_exact, d, large).astype(np.int32)


def _bias_table(rel_bias, window, dil):
    blk = ATTN_BLOCK
    rel = np.arange(blk)[:, None] + blk - np.arange(2 * blk)[None, :]
    band = (rel >= 0) & (rel <= window // dil)
    bucket = np.where(band, _t5_bucket(rel * dil), -1)[None]
    table = jnp.full((N_HEADS, blk, 2 * blk), NEG_INF, F32)
    for n in range(N_REL_BUCKETS):
        if (bucket == n).any():
            table = jnp.where(bucket == n, rel_bias[n].astype(F32)[:, None, None], table)
    return table


def _attn_kernel(q_ref, kc_ref, kp_ref, vc_ref, vp_ref, bias_ref, o_ref, *scr):
    nb = len(BRANCHES)
    k_scr, v_scr, o_scr, l_scr = scr[0:nb], scr[nb:2 * nb], scr[2 * nb:3 * nb], scr[3 * nb:4 * nb]
    blk = ATTN_BLOCK
    first_super = pl.program_id(1) == 0
    hp = pl.program_id(2)

    for bi, (_, d) in enumerate(BRANCHES):
        n_d = ATTN_SUPER // d
        for r in range(d):
            for cur, prev, dst in ((kc_ref, kp_ref, k_scr[bi]), (vc_ref, vp_ref, v_scr[bi])):
                dst[r, 0:blk, :] = prev[_rows((n_d - blk) * d + r, blk, d), :].astype(BF16)
                dst[r, blk:, :] = cur[_rows(r, n_d, d), :].astype(BF16)

    lane = lax.broadcasted_iota(jnp.int32, (blk, LANES), 1)
    low_half = lane < HEAD_DIM
    col = lax.broadcasted_iota(jnp.int32, (blk, 2 * blk), 1)
    ones = jnp.ones((2 * blk, LANES), BF16)

    for bi, (_, d) in enumerate(BRANCHES):
        q_blocks = ATTN_SUPER // d // blk

        def unit(u, carry, bi=bi, d=d, q_blocks=q_blocks):
            r = u // q_blocks
            qb = u % q_blocks
            tok = _rows(r + qb * (blk * d), blk, d)
            no_prev = jnp.logical_and(first_super, qb == 0)
            prev_mask = jnp.where(jnp.logical_and(no_prev, col < blk), NEG_INF, 0.0)
            q = q_ref[tok, :] * (HEAD_DIM ** -0.5)
            win = pl.ds(pl.multiple_of(qb * blk, blk), 2 * blk)
            kk = k_scr[bi][r, win, :]
            vv = jnp.concatenate([v_scr[bi][r, win, :], ones], axis=1)
            outs, lses = [], []
            for hh in range(2):
                qm = jnp.where(low_half if hh == 0 else jnp.logical_not(low_half), q, 0.0).astype(BF16)
                s = lax.dot_general(qm, kk, (((1,), (1,)), ((), ())), preferred_element_type=F32)
                s = s + bias_ref[bi, 2 * hp + hh] + prev_mask
                m = jnp.max(s, axis=-1, keepdims=True)
                e = jnp.exp(s - m)
                oe = _dot(e.astype(BF16), vv)
                den = oe[:, LANES:]
                outs.append(oe[:, :LANES] / den)
                lses.append(m + jnp.log(den))
            o_scr[bi][tok, :] = jnp.where(low_half, outs[0], outs[1])
            l_scr[bi][tok, :] = jnp.where(low_half, lses[0], lses[1])
            return carry

        lax.fori_loop(0, d * q_blocks, unit, 0)

    def mix(c, carry):
        rows = pl.ds(pl.multiple_of(c * 256, 256), 256)
        ls = [l[rows, :] for l in l_scr]
        m = functools.reduce(jnp.maximum, ls)
        es = [jnp.exp(l - m) for l in ls]
        num = functools.reduce(lambda a, b: a + b, [e * o[rows, :] for e, o in zip(es, o_scr)])
        o_ref[rows, :] = (num / functools.reduce(lambda a, b: a + b, es)).astype(BF16)
        return carry

    lax.fori_loop(0, ATTN_SUPER // 256, mix, 0)


def _attention(qkv, bias):
    B, S, _ = qkv.shape
    pairs = ATTN_WIDTH // LANES
    cur = lambda part: pl.BlockSpec((None, ATTN_SUPER, LANES), lambda b, s, hp: (b, s, part * pairs + hp))
    prev = lambda part: pl.BlockSpec((None, ATTN_SUPER, LANES),
                                     lambda b, s, hp: (b, jnp.maximum(s - 1, 0), part * pairs + hp))
    kv_scr = [pltpu.VMEM((d, ATTN_BLOCK + ATTN_SUPER // d, LANES), BF16) for _, d in BRANCHES]
    tok_scr = [pltpu.VMEM((ATTN_SUPER, LANES), F32) for _ in BRANCHES]
    return pl.pallas_call(
        _attn_kernel,
        grid=(B, S // ATTN_SUPER, pairs),
        in_specs=[cur(0), cur(1), prev(1), cur(2), prev(2), _resident(bias.shape)],
        out_specs=pl.BlockSpec((None, ATTN_SUPER, LANES), lambda b, s, hp: (b, s, hp)),
        out_shape=jax.ShapeDtypeStruct((B, S, ATTN_WIDTH), BF16),
        scratch_shapes=kv_scr + kv_scr + tok_scr + tok_scr,
        compiler_params=_params("parallel", "parallel", "parallel"),
        name="attn",
    )(qkv, qkv, qkv, qkv, qkv, bias)


def _sgu_kernel(z_ref, avg_ref, lng_ref, lnb_ref, w_ref, bt_ref, y_ref):
    z = _gelu(z_ref[0])
    u = z[:, :SGU_WIDTH]
    v = z[:, SGU_WIDTH:]
    avg = avg_ref[...]

    def group_mean(t):
        hi = t.astype(BF16)
        lo = (t - hi.astype(F32)).astype(BF16)
        return _dot(hi, avg) + _dot(lo, avg)

    d = v - group_mean(v)
    vn = d * lax.rsqrt(group_mean(d * d) + EPS) * lng_ref[...] + lnb_ref[...]
    lane = lax.broadcasted_iota(jnp.int32, (SGU_CHUNK, LANES), 1)
    low_half = lane < SGU_GROUP_WIDTH
    for c in range(ROW_TILE // SGU_CHUNK):
        rows = slice(c * SGU_CHUNK, (c + 1) * SGU_CHUNK)
        for p in range(SGU_GROUPS // 2):
            cols = slice(p * LANES, (p + 1) * LANES)
            vp = vn[rows, cols].astype(BF16)
            mixed = jnp.where(low_half, _dot(w_ref[2 * p], vp), _dot(w_ref[2 * p + 1], vp))
            y_ref[0, rows, cols] = u[rows, cols] * (mixed + bt_ref[:, cols])


def _sgu(z, avg, ln_g, ln_b, w_causal, b_table):
    B, S, _ = z.shape
    return pl.pallas_call(
        _sgu_kernel,
        grid=(B, S // ROW_TILE),
        in_specs=[pl.BlockSpec((1, ROW_TILE, 2 * SGU_WIDTH), lambda b, s: (b, s, 0)),
                  _resident((SGU_WIDTH, SGU_WIDTH)), _resident((1, SGU_WIDTH)), _resident((1, SGU_WIDTH)),
                  _resident((SGU_GROUPS, SGU_CHUNK, SGU_CHUNK)), _resident((SGU_CHUNK, SGU_WIDTH))],
        out_specs=pl.BlockSpec((1, ROW_TILE, SGU_WIDTH), lambda b, s: (b, s, 0)),
        out_shape=jax.ShapeDtypeStruct((B, S, SGU_WIDTH), F32),
        compiler_params=_params("parallel", "parallel"),
        name="sgu",
    )(z, avg, ln_g, ln_b, w_causal, b_table)


def _ssm_kernel(u_ref, bmat_ref, cmat_ref, a_ref, aseg_ref, d_ref, gw_ref, gb_ref, y_ref,
                perm_scr, x_scr, carry_scr, xin_scr):
    N = SSM_LANES
    slabs = SSM_WIDTH // LANES

    @pl.when(pl.program_id(1) == 0)
    def _():
        carry_scr[...] = jnp.zeros_like(carry_scr)

    for j in range(SUBLANES):
        for c in range(slabs):
            perm_scr[c, j * SSM_PITCH:j * SSM_PITCH + SSM_SEG, :] = \
                u_ref[j * SSM_SEG:(j + 1) * SSM_SEG, c * LANES:(c + 1) * LANES]
    u = jnp.concatenate(
        [jnp.concatenate([perm_scr[c, pl.ds(t, SUBLANES, stride=SSM_PITCH), :] for c in range(slabs)], axis=1)
         for t in range(SSM_SEG)], axis=0)
    x_scr[...] = _dot(u.astype(BF16), bmat_ref[...])

    a_re = jnp.broadcast_to(a_ref[0:1, :], (SUBLANES, N))
    a_im = jnp.broadcast_to(a_ref[1:2, :], (SUBLANES, N))

    def local_step(t, x):
        xr, xi = x
        rows = pl.ds(pl.multiple_of(t * SUBLANES, SUBLANES), SUBLANES)
        nr = a_re * xr - a_im * xi + x_scr[rows, 0:N]
        ni = a_re * xi + a_im * xr + x_scr[rows, N:2 * N]
        x_scr[rows, 0:N] = nr
        x_scr[rows, N:2 * N] = ni
        return nr, ni

    zeros = jnp.zeros((SUBLANES, N), F32)
    end_re, end_im = lax.fori_loop(0, SSM_SEG, local_step, (zeros, zeros), unroll=2)

    s_re, s_im = aseg_ref[0:1, :], aseg_ref[1:2, :]
    cr, ci = carry_scr[0:1, :], carry_scr[1:2, :]
    for j in range(SUBLANES):
        xin_scr[j:j + 1, 0:N] = cr
        xin_scr[j:j + 1, N:2 * N] = ci
        er, ei = end_re[j:j + 1, :], end_im[j:j + 1, :]
        cr, ci = s_re * cr - s_im * ci + er, s_re * ci + s_im * cr + ei
    carry_scr[0:1, :] = cr
    carry_scr[1:2, :] = ci

    def fix_step(t, f):
        fr, fi = f
        rows = pl.ds(pl.multiple_of(t * SUBLANES, SUBLANES), SUBLANES)
        nr = a_re * fr - a_im * fi
        ni = a_re * fi + a_im * fr
        x_scr[rows, 0:N] = x_scr[rows, 0:N] + nr
        x_scr[rows, N:2 * N] = x_scr[rows, N:2 * N] + ni
        return nr, ni

    lax.fori_loop(0, SSM_SEG, fix_step, (xin_scr[:, 0:N], xin_scr[:, N:2 * N]), unroll=2)

    y = _dot(x_scr[...].astype(BF16), cmat_ref[...]) + d_ref[...] * u
    y = _gelu(y)
    y = y * _sigmoid(_dot(y.astype(BF16), gw_ref[...]) + gb_ref[...])
    for t in range(SSM_SEG):
        for c in range(slabs):
            perm_scr[c, pl.ds(t, SUBLANES, stride=SSM_PITCH), :] = \
                y[t * SUBLANES:(t + 1) * SUBLANES, c * LANES:(c + 1) * LANES]
    for j in range(SUBLANES):
        for c in range(slabs):
            y_ref[j * SSM_SEG:(j + 1) * SSM_SEG, c * LANES:(c + 1) * LANES] = \
                perm_scr[c, j * SSM_PITCH:j * SSM_PITCH + SSM_SEG, :]


def _ssm(u, bmat, cmat, a_step, a_seg, d_skip, glu_w, glu_b):
    B, S, _ = u.shape
    blk = pl.BlockSpec((None, SSM_TILE, SSM_WIDTH), lambda b, s: (b, s, 0))
    return pl.pallas_call(
        _ssm_kernel,
        grid=(B, S // SSM_TILE),
        in_specs=[blk, _resident((SSM_WIDTH, 2 * SSM_LANES)), _resident((2 * SSM_LANES, SSM_WIDTH)),
                  _resident((2, SSM_LANES)), _resident((2, SSM_LANES)), _resident((1, SSM_WIDTH)),
                  _resident((SSM_WIDTH, SSM_WIDTH)), _resident((1, SSM_WIDTH))],
        out_specs=blk,
        out_shape=jax.ShapeDtypeStruct(u.shape, F32),
        scratch_shapes=[pltpu.VMEM((SSM_WIDTH // LANES, SUBLANES * SSM_PITCH, LANES), F32),
                        pltpu.VMEM((SSM_TILE, 2 * SSM_LANES), F32), pltpu.VMEM((2, SSM_LANES), F32),
                        pltpu.VMEM((SUBLANES, 2 * SSM_LANES), F32)],
        compiler_params=_params("parallel", "arbitrary"),
        name="ssm",
    )(u, bmat, cmat, a_step, a_seg, d_skip, glu_w, glu_b)


def _ssm_matrices(a_re, a_im, log_dt, b_re, b_im, c_re, c_im):
    G, N, C = SSM_GROUPS, SSM_STATE, SSM_GROUP_CH
    dt = jnp.exp(log_dt)[:, None]
    mag = jnp.exp(a_re * dt)
    ab_re, ab_im = mag * jnp.cos(a_im * dt), mag * jnp.sin(a_im * dt)
    den = a_re * a_re + a_im * a_im
    f_re = ((ab_re - 1.0) * a_re + ab_im * a_im) / den
    f_im = (ab_im * a_re - (ab_re - 1.0) * a_im) / den
    bb_re = f_re[:, :, None] * b_re - f_im[:, :, None] * b_im
    bb_im = f_re[:, :, None] * b_im + f_im[:, :, None] * b_re
    eye = jnp.eye(G, dtype=F32)
    bd_in = lambda t: jnp.einsum('gnc,gh->gchn', t, eye).reshape(G * C, G * N)
    bd_out = lambda t: jnp.einsum('gcn,gh->gnhc', t, eye).reshape(G * N, G * C)
    bmat = jnp.concatenate([bd_in(bb_re), bd_in(bb_im)], axis=1).astype(BF16)
    cmat = jnp.concatenate([bd_out(c_re), -bd_out(c_im)], axis=0).astype(BF16)
    a_step = jnp.stack([ab_re.reshape(-1), ab_im.reshape(-1)])
    pr, pi = ab_re, ab_im
    for _ in range(int(math.log2(SSM_SEG))):
        pr, pi = pr * pr - pi * pi, 2.0 * pr * pi
    a_seg = jnp.stack([pr.reshape(-1), pi.reshape(-1)])
    return bmat, cmat, a_step, a_seg


def _outproj_kernel(attn_ref, sgu_ref, ssm_ref, h_ref, g_ref, w_ref, out_ref):
    acc = h_ref[0]
    lo = 0
    for part in (attn_ref[0].astype(F32), sgu_ref[0], ssm_ref[0]):
        hi = lo + part.shape[-1]
        acc = acc + _dot((_rms_scale(part) * g_ref[:, lo:hi]).astype(BF16), w_ref[lo:hi, :])
        lo = hi
    out_ref[0] = acc


def _outproj(y_attn, y_sgu, y_ssm, h, g, w):
    B, S, _ = h.shape
    tile = lambda width: pl.BlockSpec((1, ROW_TILE, width), lambda b, s: (b, s, 0))
    return pl.pallas_call(
        _outproj_kernel,
        grid=(B, S // ROW_TILE),
        in_specs=[tile(ATTN_WIDTH), tile(SGU_WIDTH), tile(SSM_WIDTH), tile(D_MODEL),
                  _resident((1, D_MODEL)), _resident((D_MODEL, D_MODEL))],
        out_specs=tile(D_MODEL),
        out_shape=jax.ShapeDtypeStruct(h.shape, F32),
        compiler_params=_params("parallel", "parallel"),
        name="outproj",
    )(y_attn, y_sgu, y_ssm, h, g, w)


def _ffn_kernel(h_ref, p_ref, gffn_ref, wup_ref, cw_ref, cb_ref, wdown_ref, gple_ref, wgate_ref, wproj_ref,
                gfin_ref, out_ref, xn_scr, tail_scr, *, final):
    n_chunks = D_FF // FF_CHUNK
    first_tile = pl.program_id(1) == 0
    h = h_ref[0]
    xn_scr[...] = (_rms_scale(h) * gffn_ref[...]).astype(BF16)
    out_ref[0] = h

    def conv(t, col0, slot):
        cols = slice(col0, col0 + FF_CHUNK)
        tail = jnp.where(first_tile, 0.0, tail_scr[slot])
        ext = jnp.concatenate([tail, t], axis=0)
        m1 = pltpu.roll(ext, 1, axis=0)[SUBLANES:]
        m2 = pltpu.roll(ext, 2, axis=0)[SUBLANES:]
        tail_scr[slot] = t[ROW_TILE - SUBLANES:]
        return cw_ref[2:3, cols] * t + cw_ref[1:2, cols] * m1 + cw_ref[0:1, cols] * m2 + cb_ref[:, cols]

    for c in range(n_chunks):
        v0, g0 = c * FF_CHUNK, D_FF + c * FF_CHUNK
        xn = xn_scr[...]
        val = conv(_dot(xn, wup_ref[:, v0:v0 + FF_CHUNK]), v0, 2 * c)
        gate = conv(_dot(xn, wup_ref[:, g0:g0 + FF_CHUNK]), g0, 2 * c + 1)
        act = (_gelu(gate) * val).astype(BF16)
        out_ref[0] += _dot(act, wdown_ref[c * FF_CHUNK:(c + 1) * FF_CHUNK, :])

    h2 = out_ref[0]
    gate = _sigmoid(_dot((_rms_scale(h2) * gple_ref[...]).astype(BF16), wgate_ref[...]))
    h3 = h2 + gate * _dot(p_ref[...].astype(BF16), wproj_ref[...])
    if final:
        h3 = _rms_scale(h3) * gfin_ref[...]
    out_ref[0] = h3


def _ffn(h, p, layer, g_ffn, w_up, conv_w, conv_b, w_down, g_ple, w_gate, w_proj, g_final, final):
    B, S, _ = h.shape
    tile = lambda width: pl.BlockSpec((1, ROW_TILE, width), lambda b, s: (b, s, 0))
    return pl.pallas_call(
        functools.partial(_ffn_kernel, final=final),
        grid=(B, S // ROW_TILE),
        in_specs=[tile(D_MODEL), pl.BlockSpec((None, None, ROW_TILE, PLE_DIM), lambda b, s: (layer, b, s, 0)),
                  _resident((1, D_MODEL)), _resident((D_MODEL, 2 * D_FF)),
                  _resident((3, 2 * D_FF)), _resident((1, 2 * D_FF)), _resident((D_FF, D_MODEL)),
                  _resident((1, D_MODEL)), _resident((D_MODEL, D_MODEL)), _resident((PLE_DIM, D_MODEL)),
                  _resident((1, D_MODEL))],
        out_specs=tile(D_MODEL),
        out_shape=jax.ShapeDtypeStruct(h.shape, F32),
        scratch_shapes=[pltpu.VMEM((ROW_TILE, D_MODEL), BF16),
                        pltpu.VMEM((2 * (D_FF // FF_CHUNK), SUBLANES, FF_CHUNK), F32)],
        compiler_params=_params("parallel", "arbitrary"),
        name="ffn",
    )(h, p, g_ffn, w_up, conv_w, conv_b, w_down, g_ple, w_gate, w_proj, g_final)


def kernel(x, p, rel_bias, norm_attn_g, w_in, sgu_ln_g, sgu_ln_b, sgu_w, sgu_b, ssm_a_re, ssm_a_im, ssm_log_dt, ssm_b_re, ssm_b_im, ssm_c_re, ssm_c_im, ssm_d, ssm_glu_w, ssm_glu_b, branch_norm_g, w_out, norm_ffn_g, ffn_w_up, ffn_conv_w, ffn_conv_b, ffn_w_down, norm_ple_g, ple_w_gate, ple_w_proj, final_norm_g):
    row = lambda t: t.reshape(1, -1).astype(F32)
    bias = jnp.stack([_bias_table(rel_bias, window, dil) for window, dil in BRANCHES])
    group_avg = jnp.asarray(np.kron(np.eye(SGU_GROUPS), np.full((SGU_GROUP_WIDTH,) * 2, 1.0 / SGU_GROUP_WIDTH)), BF16)
    causal = np.tril(np.ones((SGU_CHUNK, SGU_CHUNK), dtype=bool))

    h = x
    for i in range(DEPTH):
        qkv, z_sgu, z_ssm = _inproj(h, row(norm_attn_g[i]), w_in[i].astype(BF16))

        y_attn = _attention(qkv, bias)

        w_causal = jnp.where(causal, sgu_w[i].astype(F32), 0.0).astype(BF16)
        b_table = jnp.repeat(sgu_b[i].astype(F32).T, SGU_GROUP_WIDTH, axis=1)
        y_sgu = _sgu(z_sgu, group_avg, row(sgu_ln_g[i]), row(sgu_ln_b[i]), w_causal, b_table)

        bmat, cmat, a_step, a_seg = _ssm_matrices(ssm_a_re[i], ssm_a_im[i], ssm_log_dt[i], ssm_b_re[i],
                                                  ssm_b_im[i], ssm_c_re[i], ssm_c_im[i])
        y_ssm = _ssm(z_ssm, bmat, cmat, a_step, a_seg, row(ssm_d[i]), ssm_glu_w[i].astype(BF16),
                     row(ssm_glu_b[i]))

        h = _outproj(y_attn, y_sgu, y_ssm, h, row(branch_norm_g[i]), w_out[i].astype(BF16))

        h = _ffn(h, p, i, row(norm_ffn_g[i]), ffn_w_up[i].astype(BF16), ffn_conv_w[i].astype(F32),
                 row(ffn_conv_b[i]), ffn_w_down[i].astype(BF16), row(norm_ple_g[i]),
                 ple_w_gate[i].astype(BF16), ple_w_proj[i].astype(BF16), row(final_norm_g),
                 final=(i == DEPTH - 1))
    return h
```

```python
import functools
import math

import numpy as np
import jax
import jax.numpy as jnp
from jax import lax
from jax.experimental import pallas as pl
from jax.experimental.pallas import tpu as pltpu

D_MODEL = 1024
DEPTH = 2
PLE_DIM = 256
HEAD_DIM = 64
N_HEADS = 8
ATTN_WIDTH = N_HEADS * HEAD_DIM
BRANCHES = ((128, 1), (512, 4), (2048, 16))
ATTN_BLOCK = 128
N_REL_BUCKETS = 32
REL_MAX_DISTANCE = 2048
SGU_GROUPS = 4
SGU_GROUP_WIDTH = 64
SGU_WIDTH = SGU_GROUPS * SGU_GROUP_WIDTH
SGU_CHUNK = 128
SSM_GROUP_CH = 16
SSM_WIDTH = 256
SSM_GROUPS = SSM_WIDTH // SSM_GROUP_CH
SSM_STATE = 64
SSM_LANES = SSM_GROUPS * SSM_STATE
QKV_WIDTH = 3 * ATTN_WIDTH
IN_WIDTH = QKV_WIDTH + 2 * SGU_WIDTH + SSM_WIDTH
D_FF = 2816
EPS = 1e-6
NEG_INF = -1e30
LOG2E = 1.4426950408889634

LANES = 128
SUBLANES = 8
VMEM_LIMIT = 56 * 1024 * 1024

ROW_TILE = 512
ATTN_SUPER = ATTN_BLOCK * max(d for _, d in BRANCHES)
SSM_SEG = 64
SSM_TILE = SUBLANES * SSM_SEG
SSM_PITCH = SSM_SEG + SUBLANES
FF_CHUNK = 256

BF16 = jnp.bfloat16
F32 = jnp.float32


def _gelu(x):
    return 0.5 * x * (1.0 + jnp.tanh(0.7978845608028654 * (x + 0.044715 * (x * x * x))))


def _sigmoid(x):
    return 1.0 / (1.0 + jnp.exp(-x))


def _rms_scale(x):
    return x * lax.rsqrt(jnp.mean(x * x, axis=-1, keepdims=True) + EPS)


def _dot(a, b):
    return jnp.dot(a, b, preferred_element_type=F32)


def _params(*sem):
    return pltpu.CompilerParams(dimension_semantics=sem, vmem_limit_bytes=VMEM_LIMIT)


def _resident(shape):
    nd = len(shape)
    return pl.BlockSpec(shape, lambda *_: (0,) * nd, pipeline_mode=pl.Buffered(1))


def _rows(start, size, stride):
    return pl.ds(start, size) if stride == 1 else pl.ds(start, size, stride=stride)


def _inproj_kernel(x_ref, g_ref, w_ref, qkv_ref, sgu_ref, ssm_ref):
    xn = (_rms_scale(x_ref[0]) * g_ref[...]).astype(BF16)
    for c in range(0, QKV_WIDTH, 512):
        qkv_ref[0, :, c:c + 512] = _dot(xn, w_ref[:, c:c + 512])
    for c in range(0, 2 * SGU_WIDTH, 256):
        sgu_ref[0, :, c:c + 256] = _dot(xn, w_ref[:, QKV_WIDTH + c:QKV_WIDTH + c + 256])
    ssm_ref[0] = _dot(xn, w_ref[:, QKV_WIDTH + 2 * SGU_WIDTH:])


def _inproj(h, g, w):
    B, S, _ = h.shape
    return pl.pallas_call(
        _inproj_kernel,
        grid=(B, S // ROW_TILE),
        in_specs=[pl.BlockSpec((1, ROW_TILE, D_MODEL), lambda b, s: (b, s, 0)),
                  _resident((1, D_MODEL)), _resident((D_MODEL, IN_WIDTH))],
        out_specs=[pl.BlockSpec((1, ROW_TILE, QKV_WIDTH), lambda b, s: (b, s, 0)),
                   pl.BlockSpec((1, ROW_TILE, 2 * SGU_WIDTH), lambda b, s: (b, s, 0)),
                   pl.BlockSpec((1, ROW_TILE, SSM_WIDTH), lambda b, s: (b, s, 0))],
        out_shape=[jax.ShapeDtypeStruct((B, S, QKV_WIDTH), F32),
                   jax.ShapeDtypeStruct((B, S, 2 * SGU_WIDTH), F32),
                   jax.ShapeDtypeStruct((B, S, SSM_WIDTH), F32)],
        compiler_params=_params("parallel", "parallel"),
        name="inproj",
    )(h, g, w)


def _t5_bucket(dist):
    max_exact = N_REL_BUCKETS // 2
    d = np.maximum(dist, 0)
    large = max_exact + (np.log(np.maximum(d, 1) / max_exact)
                         / np.log(REL_MAX_DISTANCE / max_exact)
                         * (N_REL_BUCKETS - max_exact)).astype(np.int32)
    large = np.minimum(large, N_REL_BUCKETS - 1)
    return np.where(d < max_exact, d, large).astype(np.int32)


def _bias_table(rel_bias, window, dil):
    blk = ATTN_BLOCK
    rel = np.arange(blk)[:, None] + blk - np.arange(2 * blk)[None, :]
    band = (rel >= 0) & (rel <= window // dil)
    bucket = np.where(band, _t5_bucket(rel * dil), -1)[None]
    table = jnp.full((N_HEADS, blk, 2 * blk), NEG_INF, F32)
    for n in range(N_REL_BUCKETS):
        if (bucket == n).any():
            table = jnp.where(bucket == n, (rel_bias[n].astype(F32) * LOG2E)[:, None, None], table)
    no_prev = jnp.where(np.arange(2 * blk)[None, None, :] < blk, NEG_INF, table)
    return jnp.stack([table, no_prev])


def _attn_kernel(q_ref, k_ref, v_ref, bias_ref, o_ref, *scr):
    nb = len(BRANCHES)
    k_scr, v_scr, o_scr, l_scr = scr[0:nb], scr[nb:2 * nb], scr[2 * nb:3 * nb], scr[3 * nb:4 * nb]
    blk = ATTN_BLOCK
    first_super = pl.program_id(2) == 0
    hp = pl.program_id(1)

    for bi, (_, d) in enumerate(BRANCHES):
        n_d = ATTN_SUPER // d
        for src, dst in ((k_ref, k_scr[bi]), (v_ref, v_scr[bi])):
            @pl.when(first_super)
            def _(dst=dst):
                dst[:, 0:blk, :] = jnp.zeros((d, blk, LANES), BF16)

            @pl.when(jnp.logical_not(first_super))
            def _(dst=dst):
                dst[:, 0:blk, :] = dst[:, n_d:n_d + blk, :]

            for r in range(d):
                dst[r, blk:, :] = src[_rows(r, n_d, d), :].astype(BF16)

    lane = lax.broadcasted_iota(jnp.int32, (blk, LANES), 1)
    low_half = lane < HEAD_DIM
    lane2 = lax.broadcasted_iota(jnp.int32, (2 * blk, LANES), 1)
    low2 = lane2 < HEAD_DIM
    zero_kv = jnp.zeros((2 * blk, LANES), BF16)
    ones_low = jnp.where(low2, 1.0, 0.0).astype(BF16)
    ones_high = jnp.where(low2, 0.0, 1.0).astype(BF16)

    for bi, (_, d) in enumerate(BRANCHES):
        q_blocks = ATTN_SUPER // d // blk

        def unit(u, carry, bi=bi, d=d, q_blocks=q_blocks):
            r = u // q_blocks
            qb = u % q_blocks
            tok = _rows(r + qb * (blk * d), blk, d)
            variant = jnp.logical_and(first_super, qb == 0).astype(jnp.int32)
            q = q_ref[tok, :] * (HEAD_DIM ** -0.5 * LOG2E)
            q2 = jnp.concatenate([jnp.where(low_half, q, 0.0), jnp.where(low_half, 0.0, q)], axis=0).astype(BF16)
            win = pl.ds(pl.multiple_of(qb * blk, blk), 2 * blk)
            kk = k_scr[bi][r, win, :]
            vv = v_scr[bi][r, win, :]
            s = lax.dot_general(q2, kk, (((1,), (1,)), ((), ())), preferred_element_type=F32)
            s = s + jnp.concatenate([bias_ref[bi, variant, 2 * hp], bias_ref[bi, variant, 2 * hp + 1]], axis=0)
            m = jnp.max(s, axis=-1, keepdims=True)
            e = jnp.exp2(s - m).astype(BF16)
            e2 = jnp.concatenate([e[:blk], e[blk:]], axis=1)
            v2 = jnp.concatenate(
                [jnp.concatenate([jnp.where(low2, vv, zero_kv), ones_low], axis=1),
                 jnp.concatenate([jnp.where(low2, zero_kv, vv), ones_high], axis=1)], axis=0)
            oe = _dot(e2, v2)
            den = oe[:, LANES:]
            o_scr[bi][tok, :] = oe[:, :LANES] / den
            l_scr[bi][tok, :] = jnp.where(low_half, m[:blk], m[blk:]) + jnp.log2(den)
            return carry

        lax.fori_loop(0, d * q_blocks, unit, 0, unroll=16)

    def mix(c, carry):
        rows = pl.ds(pl.multiple_of(c * 256, 256), 256)
        ls = [l[rows, :] for l in l_scr]
        m = functools.reduce(jnp.maximum, ls)
        es = [jnp.exp2(l - m) for l in ls]
        num = functools.reduce(lambda a, b: a + b, [e * o[rows, :] for e, o in zip(es, o_scr)])
        o_ref[rows, :] = (num / functools.reduce(lambda a, b: a + b, es)).astype(BF16)
        return carry

    lax.fori_loop(0, ATTN_SUPER // 256, mix, 0)


def _attention(qkv, bias):
    B, S, _ = qkv.shape
    pairs = ATTN_WIDTH // LANES
    slab = lambda part: pl.BlockSpec((None, ATTN_SUPER, LANES), lambda b, hp, s: (b, s, part * pairs + hp))
    kv_scr = [pltpu.VMEM((d, ATTN_BLOCK + ATTN_SUPER // d, LANES), BF16) for _, d in BRANCHES]
    tok_scr = [pltpu.VMEM((ATTN_SUPER, LANES), F32) for _ in BRANCHES]
    return pl.pallas_call(
        _attn_kernel,
        grid=(B, pairs, S // ATTN_SUPER),
        in_specs=[slab(0), slab(1), slab(2), _resident(bias.shape)],
        out_specs=pl.BlockSpec((None, ATTN_SUPER, LANES), lambda b, hp, s: (b, s, hp)),
        out_shape=jax.ShapeDtypeStruct((B, S, ATTN_WIDTH), BF16),
        scratch_shapes=kv_scr + kv_scr + tok_scr + tok_scr,
        compiler_params=_params("parallel", "parallel", "arbitrary"),
        name="attn",
    )(qkv, qkv, qkv, bias)


def _sgu_kernel(z_ref, avg_ref, lng_ref, lnb_ref, w_ref, bt_ref, y_ref):
    z = _gelu(z_ref[0])
    u = z[:, :SGU_WIDTH]
    v = z[:, SGU_WIDTH:]
    avg = avg_ref[...]

    def group_mean(t):
        hi = t.astype(BF16)
        lo = (t - hi.astype(F32)).astype(BF16)
        return _dot(hi, avg) + _dot(lo, avg)

    d = v - group_mean(v)
    vn = d * lax.rsqrt(group_mean(d * d) + EPS) * lng_ref[...] + lnb_ref[...]
    lane = lax.broadcasted_iota(jnp.int32, (SGU_CHUNK, LANES), 1)
    low_half = lane < SGU_GROUP_WIDTH
    for c in range(ROW_TILE // SGU_CHUNK):
        rows = slice(c * SGU_CHUNK, (c + 1) * SGU_CHUNK)
        for p in range(SGU_GROUPS // 2):
            cols = slice(p * LANES, (p + 1) * LANES)
            vp = vn[rows, cols].astype(BF16)
            mixed = jnp.where(low_half, _dot(w_ref[2 * p], vp), _dot(w_ref[2 * p + 1], vp))
            y_ref[0, rows, cols] = u[rows, cols] * (mixed + bt_ref[:, cols])


def _sgu(z, avg, ln_g, ln_b, w_causal, b_table):
    B, S, _ = z.shape
    return pl.pallas_call(
        _sgu_kernel,
        grid=(B, S // ROW_TILE),
        in_specs=[pl.BlockSpec((1, ROW_TILE, 2 * SGU_WIDTH), lambda b, s: (b, s, 0)),
                  _resident((SGU_WIDTH, SGU_WIDTH)), _resident((1, SGU_WIDTH)), _resident((1, SGU_WIDTH)),
                  _resident((SGU_GROUPS, SGU_CHUNK, SGU_CHUNK)), _resident((SGU_CHUNK, SGU_WIDTH))],
        out_specs=pl.BlockSpec((1, ROW_TILE, SGU_WIDTH), lambda b, s: (b, s, 0)),
        out_shape=jax.ShapeDtypeStruct((B, S, SGU_WIDTH), F32),
        compiler_params=_params("parallel", "parallel"),
        name="sgu",
    )(z, avg, ln_g, ln_b, w_causal, b_table)


def _ssm_kernel(u_ref, bmat_ref, cmat_ref, a_ref, aseg_ref, d_ref, gw_ref, gb_ref, y_ref,
                perm_scr, x_scr, carry_scr, xin_scr):
    N = SSM_LANES
    slabs = SSM_WIDTH // LANES

    @pl.when(pl.program_id(1) == 0)
    def _():
        carry_scr[...] = jnp.zeros_like(carry_scr)

    for j in range(SUBLANES):
        for c in range(slabs):
            perm_scr[c, j * SSM_PITCH:j * SSM_PITCH + SSM_SEG, :] = \
                u_ref[j * SSM_SEG:(j + 1) * SSM_SEG, c * LANES:(c + 1) * LANES]
    u = jnp.concatenate(
        [jnp.concatenate([perm_scr[c, pl.ds(t, SUBLANES, stride=SSM_PITCH), :] for c in range(slabs)], axis=1)
         for t in range(SSM_SEG)], axis=0)
    x_scr[...] = _dot(u.astype(BF16), bmat_ref[...])

    a_re = jnp.broadcast_to(a_ref[0:1, :], (SUBLANES, N))
    a_im = jnp.broadcast_to(a_ref[1:2, :], (SUBLANES, N))

    def local_step(t, x):
        xr, xi = x
        rows = pl.ds(pl.multiple_of(t * SUBLANES, SUBLANES), SUBLANES)
        nr = a_re * xr - a_im * xi + x_scr[rows, 0:N]
        ni = a_re * xi + a_im * xr + x_scr[rows, N:2 * N]
        x_scr[rows, 0:N] = nr
        x_scr[rows, N:2 * N] = ni
        return nr, ni

    zeros = jnp.zeros((SUBLANES, N), F32)
    end_re, end_im = lax.fori_loop(0, SSM_SEG, local_step, (zeros, zeros), unroll=2)

    s_re, s_im = aseg_ref[0:1, :], aseg_ref[1:2, :]
    cr, ci = carry_scr[0:1, :], carry_scr[1:2, :]
    for j in range(SUBLANES):
        xin_scr[j:j + 1, 0:N] = cr
        xin_scr[j:j + 1, N:2 * N] = ci
        er, ei = end_re[j:j + 1, :], end_im[j:j + 1, :]
        cr, ci = s_re * cr - s_im * ci + er, s_re * ci + s_im * cr + ei
    carry_scr[0:1, :] = cr
    carry_scr[1:2, :] = ci

    def fix_step(t, f):
        fr, fi = f
        rows = pl.ds(pl.multiple_of(t * SUBLANES, SUBLANES), SUBLANES)
        nr = a_re * fr - a_im * fi
        ni = a_re * fi + a_im * fr
        x_scr[rows, 0:N] = x_scr[rows, 0:N] + nr
        x_scr[rows, N:2 * N] = x_scr[rows, N:2 * N] + ni
        return nr, ni

    lax.fori_loop(0, SSM_SEG, fix_step, (xin_scr[:, 0:N], xin_scr[:, N:2 * N]), unroll=2)

    y = _dot(x_scr[...].astype(BF16), cmat_ref[...]) + d_ref[...] * u
    y = _gelu(y)
    y = y * _sigmoid(_dot(y.astype(BF16), gw_ref[...]) + gb_ref[...])
    for t in range(SSM_SEG):
        for c in range(slabs):
            perm_scr[c, pl.ds(t, SUBLANES, stride=SSM_PITCH), :] = \
                y[t * SUBLANES:(t + 1) * SUBLANES, c * LANES:(c + 1) * LANES]
    for j in range(SUBLANES):
        for c in range(slabs):
            y_ref[j * SSM_SEG:(j + 1) * SSM_SEG, c * LANES:(c + 1) * LANES] = \
                perm_scr[c, j * SSM_PITCH:j * SSM_PITCH + SSM_SEG, :]


def _ssm(u, bmat, cmat, a_step, a_seg, d_skip, glu_w, glu_b):
    B, S, _ = u.shape
    blk = pl.BlockSpec((None, SSM_TILE, SSM_WIDTH), lambda b, s: (b, s, 0))
    return pl.pallas_call(
        _ssm_kernel,
        grid=(B, S // SSM_TILE),
        in_specs=[blk, _resident((SSM_WIDTH, 2 * SSM_LANES)), _resident((2 * SSM_LANES, SSM_WIDTH)),
                  _resident((2, SSM_LANES)), _resident((2, SSM_LANES)), _resident((1, SSM_WIDTH)),
                  _resident((SSM_WIDTH, SSM_WIDTH)), _resident((1, SSM_WIDTH))],
        out_specs=blk,
        out_shape=jax.ShapeDtypeStruct(u.shape, F32),
        scratch_shapes=[pltpu.VMEM((SSM_WIDTH // LANES, SUBLANES * SSM_PITCH, LANES), F32),
                        pltpu.VMEM((SSM_TILE, 2 * SSM_LANES), F32), pltpu.VMEM((2, SSM_LANES), F32),
                        pltpu.VMEM((SUBLANES, 2 * SSM_LANES), F32)],
        compiler_params=_params("parallel", "arbitrary"),
        name="ssm",
    )(u, bmat, cmat, a_step, a_seg, d_skip, glu_w, glu_b)


def _ssm_matrices(a_re, a_im, log_dt, b_re, b_im, c_re, c_im):
    G, N, C = SSM_GROUPS, SSM_STATE, SSM_GROUP_CH
    dt = jnp.exp(log_dt)[:, None]
    mag = jnp.exp(a_re * dt)
    ab_re, ab_im = mag * jnp.cos(a_im * dt), mag * jnp.sin(a_im * dt)
    den = a_re * a_re + a_im * a_im
    f_re = ((ab_re - 1.0) * a_re + ab_im * a_im) / den
    f_im = (ab_im * a_re - (ab_re - 1.0) * a_im) / den
    bb_re = f_re[:, :, None] * b_re - f_im[:, :, None] * b_im
    bb_im = f_re[:, :, None] * b_im + f_im[:, :, None] * b_re
    eye = jnp.eye(G, dtype=F32)
    bd_in = lambda t: jnp.einsum('gnc,gh->gchn', t, eye).reshape(G * C, G * N)
    bd_out = lambda t: jnp.einsum('gcn,gh->gnhc', t, eye).reshape(G * N, G * C)
    bmat = jnp.concatenate([bd_in(bb_re), bd_in(bb_im)], axis=1).astype(BF16)
    cmat = jnp.concatenate([bd_out(c_re), -bd_out(c_im)], axis=0).astype(BF16)
    a_step = jnp.stack([ab_re.reshape(-1), ab_im.reshape(-1)])
    pr, pi = ab_re, ab_im
    for _ in range(int(math.log2(SSM_SEG))):
        pr, pi = pr * pr - pi * pi, 2.0 * pr * pi
    a_seg = jnp.stack([pr.reshape(-1), pi.reshape(-1)])
    return bmat, cmat, a_step, a_seg


def _outproj_kernel(attn_ref, sgu_ref, ssm_ref, h_ref, g_ref, w_ref, out_ref):
    acc = h_ref[0]
    lo = 0
    for part in (attn_ref[0].astype(F32), sgu_ref[0], ssm_ref[0]):
        hi = lo + part.shape[-1]
        acc = acc + _dot((_rms_scale(part) * g_ref[:, lo:hi]).astype(BF16), w_ref[lo:hi, :])
        lo = hi
    out_ref[0] = acc


def _outproj(y_attn, y_sgu, y_ssm, h, g, w):
    B, S, _ = h.shape
    tile = lambda width: pl.BlockSpec((1, ROW_TILE, width), lambda b, s: (b, s, 0))
    return pl.pallas_call(
        _outproj_kernel,
        grid=(B, S // ROW_TILE),
        in_specs=[tile(ATTN_WIDTH), tile(SGU_WIDTH), tile(SSM_WIDTH), tile(D_MODEL),
                  _resident((1, D_MODEL)), _resident((D_MODEL, D_MODEL))],
        out_specs=tile(D_MODEL),
        out_shape=jax.ShapeDtypeStruct(h.shape, F32),
        compiler_params=_params("parallel", "parallel"),
        name="outproj",
    )(y_attn, y_sgu, y_ssm, h, g, w)


def _ffn_kernel(h_ref, p_ref, gffn_ref, wup_ref, cw_ref, cb_ref, wdown_ref, gple_ref, wgate_ref, wproj_ref,
                gfin_ref, out_ref, xn_scr, tail_scr, *, final):
    n_chunks = D_FF // FF_CHUNK
    first_tile = pl.program_id(1) == 0
    h = h_ref[0]
    xn_scr[...] = (_rms_scale(h) * gffn_ref[...]).astype(BF16)
    out_ref[0] = h

    def conv(t, col0, slot):
        cols = slice(col0, col0 + FF_CHUNK)
        tail = jnp.where(first_tile, 0.0, tail_scr[slot])
        ext = jnp.concatenate([tail, t], axis=0)
        m1 = pltpu.roll(ext, 1, axis=0)[SUBLANES:]
        m2 = pltpu.roll(ext, 2, axis=0)[SUBLANES:]
        tail_scr[slot] = t[ROW_TILE - SUBLANES:]
        return cw_ref[2:3, cols] * t + cw_ref[1:2, cols] * m1 + cw_ref[0:1, cols] * m2 + cb_ref[:, cols]

    for c in range(n_chunks):
        v0, g0 = c * FF_CHUNK, D_FF + c * FF_CHUNK
        xn = xn_scr[...]
        val = conv(_dot(xn, wup_ref[:, v0:v0 + FF_CHUNK]), v0, 2 * c)
        gate = conv(_dot(xn, wup_ref[:, g0:g0 + FF_CHUNK]), g0, 2 * c + 1)
        act = (_gelu(gate) * val).astype(BF16)
        out_ref[0] += _dot(act, wdown_ref[c * FF_CHUNK:(c + 1) * FF_CHUNK, :])

    h2 = out_ref[0]
    gate = _sigmoid(_dot((_rms_scale(h2) * gple_ref[...]).astype(BF16), wgate_ref[...]))
    h3 = h2 + gate * _dot(p_ref[...].astype(BF16), wproj_ref[...])
    if final:
        h3 = _rms_scale(h3) * gfin_ref[...]
    out_ref[0] = h3


def _ffn(h, p, layer, g_ffn, w_up, conv_w, conv_b, w_down, g_ple, w_gate, w_proj, g_final, final):
    B, S, _ = h.shape
    tile = lambda width: pl.BlockSpec((1, ROW_TILE, width), lambda b, s: (b, s, 0))
    return pl.pallas_call(
        functools.partial(_ffn_kernel, final=final),
        grid=(B, S // ROW_TILE),
        in_specs=[tile(D_MODEL), pl.BlockSpec((None, None, ROW_TILE, PLE_DIM), lambda b, s: (layer, b, s, 0)),
                  _resident((1, D_MODEL)), _resident((D_MODEL, 2 * D_FF)),
                  _resident((3, 2 * D_FF)), _resident((1, 2 * D_FF)), _resident((D_FF, D_MODEL)),
                  _resident((1, D_MODEL)), _resident((D_MODEL, D_MODEL)), _resident((PLE_DIM, D_MODEL)),
                  _resident((1, D_MODEL))],
        out_specs=tile(D_MODEL),
        out_shape=jax.ShapeDtypeStruct(h.shape, F32),
        scratch_shapes=[pltpu.VMEM((ROW_TILE, D_MODEL), BF16),
                        pltpu.VMEM((2 * (D_FF // FF_CHUNK), SUBLANES, FF_CHUNK), F32)],
        compiler_params=_params("parallel", "arbitrary"),
        name="ffn",
    )(h, p, g_ffn, w_up, conv_w, conv_b, w_down, g_ple, w_gate, w_proj, g_final)


def kernel(x, p, rel_bias, norm_attn_g, w_in, sgu_ln_g, sgu_ln_b, sgu_w, sgu_b, ssm_a_re, ssm_a_im, ssm_log_dt, ssm_b_re, ssm_b_im, ssm_c_re, ssm_c_im, ssm_d, ssm_glu_w, ssm_glu_b, branch_norm_g, w_out, norm_ffn_g, ffn_w_up, ffn_conv_w, ffn_conv_b, ffn_w_down, norm_ple_g, ple_w_gate, ple_w_proj, final_norm_g):
    row = lambda t: t.reshape(1, -1).astype(F32)
    bias = jnp.stack([_bias_table(rel_bias, window, dil) for window, dil in BRANCHES])
    group_avg = jnp.asarray(np.kron(np.eye(SGU_GROUPS), np.full((SGU_GROUP_WIDTH,) * 2, 1.0 / SGU_GROUP_WIDTH)), BF16)
    causal = np.tril(np.ones((SGU_CHUNK, SGU_CHUNK), dtype=bool))

    h = x
    for i in range(DEPTH):
        qkv, z_sgu, z_ssm = _inproj(h, row(norm_attn_g[i]), w_in[i].astype(BF16))

        y_attn = _attention(qkv, bias)

        w_causal = jnp.where(causal, sgu_w[i].astype(F32), 0.0).astype(BF16)
        b_table = jnp.repeat(sgu_b[i].astype(F32).T, SGU_GROUP_WIDTH, axis=1)
        y_sgu = _sgu(z_sgu, group_avg, row(sgu_ln_g[i]), row(sgu_ln_b[i]), w_causal, b_table)

        bmat, cmat, a_step, a_seg = _ssm_matrices(ssm_a_re[i], ssm_a_im[i], ssm_log_dt[i], ssm_b_re[i],
                                                  ssm_b_im[i], ssm_c_re[i], ssm_c_im[i])
        y_ssm = _ssm(z_ssm, bmat, cmat, a_step, a_seg, row(ssm_d[i]), ssm_glu_w[i].astype(BF16),
                     row(ssm_glu_b[i]))

        h = _outproj(y_attn, y_sgu, y_ssm, h, row(branch_norm_g[i]), w_out[i].astype(BF16))

        h = _ffn(h, p, i, row(norm_ffn_g[i]), ffn_w_up[i].astype(BF16), ffn_conv_w[i].astype(F32),
                 row(ffn_conv_b[i]), ffn_w_down[i].astype(BF16), row(norm_ple_g[i]),
                 ple_w_gate[i].astype(BF16), ple_w_proj[i].astype(BF16), row(final_norm_g),
                 final=(i == DEPTH - 1))
    return h
```

```python
import functools
import math

import numpy as np
import jax
import jax.numpy as jnp
from jax import lax
from jax.experimental import pallas as pl
from jax.experimental.pallas import tpu as pltpu

D_MODEL = 1024
DEPTH = 2
PLE_DIM = 256
HEAD_DIM = 64
N_HEADS = 8
ATTN_WIDTH = N_HEADS * HEAD_DIM
BRANCHES = ((128, 1), (512, 4), (2048, 16))
ATTN_BLOCK = 128
N_REL_BUCKETS = 32
REL_MAX_DISTANCE = 2048
SGU_GROUPS = 4
SGU_GROUP_WIDTH = 64
SGU_WIDTH = SGU_GROUPS * SGU_GROUP_WIDTH
SGU_CHUNK = 128
SSM_GROUP_CH = 16
SSM_WIDTH = 256
SSM_GROUPS = SSM_WIDTH // SSM_GROUP_CH
SSM_STATE = 64
SSM_LANES = SSM_GROUPS * SSM_STATE
QKV_WIDTH = 3 * ATTN_WIDTH
IN_WIDTH = QKV_WIDTH + 2 * SGU_WIDTH + SSM_WIDTH
D_FF = 2816
EPS = 1e-6
NEG_INF = -1e30
LOG2E = 1.4426950408889634

LANES = 128
SUBLANES = 8
VMEM_LIMIT = 56 * 1024 * 1024

ROW_TILE = 512
ATTN_SUPER = ATTN_BLOCK * max(d for _, d in BRANCHES)
ATTN_STAGE_DIL = 4
SSM_SEG = 64
SSM_TILE = SUBLANES * SSM_SEG
SSM_PITCH = SSM_SEG + SUBLANES
FF_CHUNK = 256
FF_GROUP = 4
GELU_C0 = 0.7978845608028654
GELU_C1 = GELU_C0 * 0.044715

BF16 = jnp.bfloat16
F32 = jnp.float32


def _gelu(x):
    return 0.5 * x * (1.0 + jnp.tanh(0.7978845608028654 * (x + 0.044715 * (x * x * x))))


def _sigmoid(x):
    return 1.0 / (1.0 + jnp.exp(-x))


def _rms_scale(x):
    return x * lax.rsqrt(jnp.mean(x * x, axis=-1, keepdims=True) + EPS)


def _dot(a, b):
    return jnp.dot(a, b, preferred_element_type=F32)


def _params(*sem):
    return pltpu.CompilerParams(dimension_semantics=sem, vmem_limit_bytes=VMEM_LIMIT)


def _resident(shape):
    nd = len(shape)
    return pl.BlockSpec(shape, lambda *_: (0,) * nd, pipeline_mode=pl.Buffered(1))


def _rows(start, size, stride):
    return pl.ds(start, size) if stride == 1 else pl.ds(start, size, stride=stride)


def _inproj_kernel(x_ref, g_ref, w_ref, qkv_ref, sgu_ref, ssm_ref):
    xn = (_rms_scale(x_ref[0]) * g_ref[...]).astype(BF16)
    for c in range(0, QKV_WIDTH, 512):
        qkv_ref[0, :, c:c + 512] = _dot(xn, w_ref[:, c:c + 512])
    for c in range(0, 2 * SGU_WIDTH, 256):
        sgu_ref[0, :, c:c + 256] = _dot(xn, w_ref[:, QKV_WIDTH + c:QKV_WIDTH + c + 256])
    ssm_ref[0] = _dot(xn, w_ref[:, QKV_WIDTH + 2 * SGU_WIDTH:])


def _inproj(h, g, w):
    B, S, _ = h.shape
    return pl.pallas_call(
        _inproj_kernel,
        grid=(B, S // ROW_TILE),
        in_specs=[pl.BlockSpec((1, ROW_TILE, D_MODEL), lambda b, s: (b, s, 0)),
                  _resident((1, D_MODEL)), _resident((D_MODEL, IN_WIDTH))],
        out_specs=[pl.BlockSpec((1, ROW_TILE, QKV_WIDTH), lambda b, s: (b, s, 0)),
                   pl.BlockSpec((1, ROW_TILE, 2 * SGU_WIDTH), lambda b, s: (b, s, 0)),
                   pl.BlockSpec((1, ROW_TILE, SSM_WIDTH), lambda b, s: (b, s, 0))],
        out_shape=[jax.ShapeDtypeStruct((B, S, QKV_WIDTH), F32),
                   jax.ShapeDtypeStruct((B, S, 2 * SGU_WIDTH), F32),
                   jax.ShapeDtypeStruct((B, S, SSM_WIDTH), F32)],
        compiler_params=_params("parallel", "parallel"),
        name="inproj",
    )(h, g, w)


def _t5_bucket(dist):
    max_exact = N_REL_BUCKETS // 2
    d = np.maximum(dist, 0)
    large = max_exact + (np.log(np.maximum(d, 1) / max_exact)
                         / np.log(REL_MAX_DISTANCE / max_exact)
                         * (N_REL_BUCKETS - max_exact)).astype(np.int32)
    large = np.minimum(large, N_REL_BUCKETS - 1)
    return np.where(d < max_exact, d, large).astype(np.int32)


def _bias_table(rel_bias, window, dil):
    blk = ATTN_BLOCK
    rel = np.arange(blk)[:, None] + blk - np.arange(2 * blk)[None, :]
    band = (rel >= 0) & (rel <= window // dil)
    bucket = np.where(band, _t5_bucket(rel * dil), -1)[None]
    table = jnp.full((N_HEADS, blk, 2 * blk), NEG_INF, F32)
    for n in range(N_REL_BUCKETS):
        if (bucket == n).any():
            table = jnp.where(bucket == n, (rel_bias[n].astype(F32) * LOG2E)[:, None, None], table)
    no_prev = jnp.where(np.arange(2 * blk)[None, None, :] < blk, NEG_INF, table)
    return jnp.stack([table, no_prev])


def _attn_kernel(q_ref, k_ref, v_ref, bias_ref, o_ref, *scr):
    nb = len(BRANCHES)
    k_scr, v_scr, o_scr, l_scr = scr[0:nb], scr[nb:2 * nb], scr[2 * nb:3 * nb], scr[3 * nb:4 * nb]
    blk = ATTN_BLOCK
    first_super = pl.program_id(2) == 0
    hp = pl.program_id(1)

    stage = scr[4 * nb:4 * nb + 2]
    sd = ATTN_STAGE_DIL
    for bi, (_, d) in enumerate(BRANCHES):
        n_d = ATTN_SUPER // d
        for ti, (src, dst) in enumerate(((k_ref, k_scr[bi]), (v_ref, v_scr[bi]))):
            @pl.when(first_super)
            def _(dst=dst):
                dst[:, 0:blk, :] = jnp.zeros((d, blk, LANES), BF16)

            @pl.when(jnp.logical_not(first_super))
            def _(dst=dst):
                dst[:, 0:blk, :] = dst[:, n_d:n_d + blk, :]

            for r in range(d):
                if d > sd and d % sd == 0:
                    x = stage[ti][r % sd, _rows(r // sd, n_d, d // sd), :]
                else:
                    x = src[_rows(r, n_d, d), :]
                    if d == sd:
                        stage[ti][r] = x
                dst[r, blk:, :] = x.astype(BF16)

    lane = lax.broadcasted_iota(jnp.int32, (blk, LANES), 1)
    low_half = lane < HEAD_DIM
    lane2 = lax.broadcasted_iota(jnp.int32, (2 * blk, LANES), 1)
    low2 = lane2 < HEAD_DIM
    zero_kv = jnp.zeros((2 * blk, LANES), BF16)
    ones_low = jnp.where(low2, 1.0, 0.0).astype(BF16)
    ones_high = jnp.where(low2, 0.0, 1.0).astype(BF16)

    for bi, (_, d) in enumerate(BRANCHES):
        q_blocks = ATTN_SUPER // d // blk

        def unit(u, carry, bi=bi, d=d, q_blocks=q_blocks):
            r = u // q_blocks
            qb = u % q_blocks
            tok = _rows(r + qb * (blk * d), blk, d)
            variant = jnp.logical_and(first_super, qb == 0).astype(jnp.int32)
            q = q_ref[tok, :] * (HEAD_DIM ** -0.5 * LOG2E)
            q2 = jnp.concatenate([jnp.where(low_half, q, 0.0), jnp.where(low_half, 0.0, q)], axis=0).astype(BF16)
            win = pl.ds(pl.multiple_of(qb * blk, blk), 2 * blk)
            kk = k_scr[bi][r, win, :]
            vv = v_scr[bi][r, win, :]
            s = lax.dot_general(q2, kk, (((1,), (1,)), ((), ())), preferred_element_type=F32)
            s = s + jnp.concatenate([bias_ref[bi, variant, 2 * hp], bias_ref[bi, variant, 2 * hp + 1]], axis=0)
            m = jnp.max(s, axis=-1, keepdims=True)
            e = jnp.exp2(s - m).astype(BF16)
            e2 = jnp.concatenate([e[:blk], e[blk:]], axis=1)
            v2 = jnp.concatenate(
                [jnp.concatenate([jnp.where(low2, vv, zero_kv), ones_low], axis=1),
                 jnp.concatenate([jnp.where(low2, zero_kv, vv), ones_high], axis=1)], axis=0)
            oe = _dot(e2, v2)
            den = oe[:, LANES:]
            o_scr[bi][tok, :] = oe[:, :LANES] / den
            l_scr[bi][tok, :] = jnp.where(low_half, m[:blk], m[blk:]) + jnp.log2(den)
            return carry

        lax.fori_loop(0, d * q_blocks, unit, 0, unroll=16)

    def mix(c, carry):
        rows = pl.ds(pl.multiple_of(c * 256, 256), 256)
        ls = [l[rows, :] for l in l_scr]
        m = functools.reduce(jnp.maximum, ls)
        es = [jnp.exp2(l - m) for l in ls]
        num = functools.reduce(lambda a, b: a + b, [e * o[rows, :] for e, o in zip(es, o_scr)])
        o_ref[rows, :] = (num / functools.reduce(lambda a, b: a + b, es)).astype(BF16)
        return carry

    lax.fori_loop(0, ATTN_SUPER // 256, mix, 0)


def _attention(qkv, bias):
    B, S, _ = qkv.shape
    pairs = ATTN_WIDTH // LANES
    slab = lambda part: pl.BlockSpec((None, ATTN_SUPER, LANES), lambda b, hp, s: (b, s, part * pairs + hp))
    kv_scr = [pltpu.VMEM((d, ATTN_BLOCK + ATTN_SUPER // d, LANES), BF16) for _, d in BRANCHES]
    tok_scr = [pltpu.VMEM((ATTN_SUPER, LANES), F32) for _ in BRANCHES]
    stage_scr = pltpu.VMEM((ATTN_STAGE_DIL, ATTN_SUPER // ATTN_STAGE_DIL, LANES), F32)
    return pl.pallas_call(
        _attn_kernel,
        grid=(B, pairs, S // ATTN_SUPER),
        in_specs=[slab(0), slab(1), slab(2), _resident(bias.shape)],
        out_specs=pl.BlockSpec((None, ATTN_SUPER, LANES), lambda b, hp, s: (b, s, hp)),
        out_shape=jax.ShapeDtypeStruct((B, S, ATTN_WIDTH), BF16),
        scratch_shapes=kv_scr + kv_scr + tok_scr + tok_scr + [stage_scr, stage_scr],
        compiler_params=_params("parallel", "parallel", "arbitrary"),
        name="attn",
    )(qkv, qkv, qkv, bias)


def _sgu_kernel(z_ref, avg_ref, lng_ref, lnb_ref, w_ref, bt_ref, y_ref):
    z = _gelu(z_ref[0])
    u = z[:, :SGU_WIDTH]
    v = z[:, SGU_WIDTH:]
    avg = avg_ref[...]

    def group_mean(t):
        hi = t.astype(BF16)
        lo = (t - hi.astype(F32)).astype(BF16)
        return _dot(hi, avg) + _dot(lo, avg)

    d = v - group_mean(v)
    vn = d * lax.rsqrt(group_mean(d * d) + EPS) * lng_ref[...] + lnb_ref[...]
    lane = lax.broadcasted_iota(jnp.int32, (SGU_CHUNK, LANES), 1)
    low_half = lane < SGU_GROUP_WIDTH
    for c in range(ROW_TILE // SGU_CHUNK):
        rows = slice(c * SGU_CHUNK, (c + 1) * SGU_CHUNK)
        for p in range(SGU_GROUPS // 2):
            cols = slice(p * LANES, (p + 1) * LANES)
            vp = vn[rows, cols].astype(BF16)
            mixed = jnp.where(low_half, _dot(w_ref[2 * p], vp), _dot(w_ref[2 * p + 1], vp))
            y_ref[0, rows, cols] = u[rows, cols] * (mixed + bt_ref[:, cols])


def _sgu(z, avg, ln_g, ln_b, w_causal, b_table):
    B, S, _ = z.shape
    return pl.pallas_call(
        _sgu_kernel,
        grid=(B, S // ROW_TILE),
        in_specs=[pl.BlockSpec((1, ROW_TILE, 2 * SGU_WIDTH), lambda b, s: (b, s, 0)),
                  _resident((SGU_WIDTH, SGU_WIDTH)), _resident((1, SGU_WIDTH)), _resident((1, SGU_WIDTH)),
                  _resident((SGU_GROUPS, SGU_CHUNK, SGU_CHUNK)), _resident((SGU_CHUNK, SGU_WIDTH))],
        out_specs=pl.BlockSpec((1, ROW_TILE, SGU_WIDTH), lambda b, s: (b, s, 0)),
        out_shape=jax.ShapeDtypeStruct((B, S, SGU_WIDTH), F32),
        compiler_params=_params("parallel", "parallel"),
        name="sgu",
    )(z, avg, ln_g, ln_b, w_causal, b_table)


def _ssm_kernel(u_ref, bmat_ref, cmat_ref, a_ref, aseg_ref, d_ref, gw_ref, gb_ref, y_ref,
                perm_scr, x_scr, carry_scr, xin_scr):
    N = SSM_LANES
    slabs = SSM_WIDTH // LANES

    @pl.when(pl.program_id(1) == 0)
    def _():
        carry_scr[...] = jnp.zeros_like(carry_scr)

    for j in range(SUBLANES):
        for c in range(slabs):
            perm_scr[c, j * SSM_PITCH:j * SSM_PITCH + SSM_SEG, :] = \
                u_ref[j * SSM_SEG:(j + 1) * SSM_SEG, c * LANES:(c + 1) * LANES]
    u = jnp.concatenate(
        [jnp.concatenate([perm_scr[c, pl.ds(t, SUBLANES, stride=SSM_PITCH), :] for c in range(slabs)], axis=1)
         for t in range(SSM_SEG)], axis=0)
    x_scr[...] = _dot(u.astype(BF16), bmat_ref[...])

    a_re = jnp.broadcast_to(a_ref[0:1, :], (SUBLANES, N))
    a_im = jnp.broadcast_to(a_ref[1:2, :], (SUBLANES, N))

    def local_step(t, x):
        xr, xi = x
        rows = pl.ds(pl.multiple_of(t * SUBLANES, SUBLANES), SUBLANES)
        nr = a_re * xr - a_im * xi + x_scr[rows, 0:N]
        ni = a_re * xi + a_im * xr + x_scr[rows, N:2 * N]
        x_scr[rows, 0:N] = nr
        x_scr[rows, N:2 * N] = ni
        return nr, ni

    zeros = jnp.zeros((SUBLANES, N), F32)
    end_re, end_im = lax.fori_loop(0, SSM_SEG, local_step, (zeros, zeros), unroll=2)

    s_re, s_im = aseg_ref[0:1, :], aseg_ref[1:2, :]
    cr, ci = carry_scr[0:1, :], carry_scr[1:2, :]
    for j in range(SUBLANES):
        xin_scr[j:j + 1, 0:N] = cr
        xin_scr[j:j + 1, N:2 * N] = ci
        er, ei = end_re[j:j + 1, :], end_im[j:j + 1, :]
        cr, ci = s_re * cr - s_im * ci + er, s_re * ci + s_im * cr + ei
    carry_scr[0:1, :] = cr
    carry_scr[1:2, :] = ci

    def fix_step(t, f):
        fr, fi = f
        rows = pl.ds(pl.multiple_of(t * SUBLANES, SUBLANES), SUBLANES)
        nr = a_re * fr - a_im * fi
        ni = a_re * fi + a_im * fr
        x_scr[rows, 0:N] = x_scr[rows, 0:N] + nr
        x_scr[rows, N:2 * N] = x_scr[rows, N:2 * N] + ni
        return nr, ni

    lax.fori_loop(0, SSM_SEG, fix_step, (xin_scr[:, 0:N], xin_scr[:, N:2 * N]), unroll=2)

    y = _dot(x_scr[...].astype(BF16), cmat_ref[...]) + d_ref[...] * u
    y = _gelu(y)
    y = y * _sigmoid(_dot(y.astype(BF16), gw_ref[...]) + gb_ref[...])
    for t in range(SSM_SEG):
        for c in range(slabs):
            perm_scr[c, pl.ds(t, SUBLANES, stride=SSM_PITCH), :] = \
                y[t * SUBLANES:(t + 1) * SUBLANES, c * LANES:(c + 1) * LANES]
    for j in range(SUBLANES):
        for c in range(slabs):
            y_ref[j * SSM_SEG:(j + 1) * SSM_SEG, c * LANES:(c + 1) * LANES] = \
                perm_scr[c, j * SSM_PITCH:j * SSM_PITCH + SSM_SEG, :]


def _ssm(u, bmat, cmat, a_step, a_seg, d_skip, glu_w, glu_b):
    B, S, _ = u.shape
    blk = pl.BlockSpec((None, SSM_TILE, SSM_WIDTH), lambda b, s: (b, s, 0))
    return pl.pallas_call(
        _ssm_kernel,
        grid=(B, S // SSM_TILE),
        in_specs=[blk, _resident((SSM_WIDTH, 2 * SSM_LANES)), _resident((2 * SSM_LANES, SSM_WIDTH)),
                  _resident((2, SSM_LANES)), _resident((2, SSM_LANES)), _resident((1, SSM_WIDTH)),
                  _resident((SSM_WIDTH, SSM_WIDTH)), _resident((1, SSM_WIDTH))],
        out_specs=blk,
        out_shape=jax.ShapeDtypeStruct(u.shape, F32),
        scratch_shapes=[pltpu.VMEM((SSM_WIDTH // LANES, SUBLANES * SSM_PITCH, LANES), F32),
                        pltpu.VMEM((SSM_TILE, 2 * SSM_LANES), F32), pltpu.VMEM((2, SSM_LANES), F32),
                        pltpu.VMEM((SUBLANES, 2 * SSM_LANES), F32)],
        compiler_params=_params("parallel", "arbitrary"),
        name="ssm",
    )(u, bmat, cmat, a_step, a_seg, d_skip, glu_w, glu_b)


def _ssm_matrices(a_re, a_im, log_dt, b_re, b_im, c_re, c_im):
    G, N, C = SSM_GROUPS, SSM_STATE, SSM_GROUP_CH
    dt = jnp.exp(log_dt)[:, None]
    mag = jnp.exp(a_re * dt)
    ab_re, ab_im = mag * jnp.cos(a_im * dt), mag * jnp.sin(a_im * dt)
    den = a_re * a_re + a_im * a_im
    f_re = ((ab_re - 1.0) * a_re + ab_im * a_im) / den
    f_im = (ab_im * a_re - (ab_re - 1.0) * a_im) / den
    bb_re = f_re[:, :, None] * b_re - f_im[:, :, None] * b_im
    bb_im = f_re[:, :, None] * b_im + f_im[:, :, None] * b_re
    eye = jnp.eye(G, dtype=F32)
    bd_in = lambda t: jnp.einsum('gnc,gh->gchn', t, eye).reshape(G * C, G * N)
    bd_out = lambda t: jnp.einsum('gcn,gh->gnhc', t, eye).reshape(G * N, G * C)
    bmat = jnp.concatenate([bd_in(bb_re), bd_in(bb_im)], axis=1).astype(BF16)
    cmat = jnp.concatenate([bd_out(c_re), -bd_out(c_im)], axis=0).astype(BF16)
    a_step = jnp.stack([ab_re.reshape(-1), ab_im.reshape(-1)])
    pr, pi = ab_re, ab_im
    for _ in range(int(math.log2(SSM_SEG))):
        pr, pi = pr * pr - pi * pi, 2.0 * pr * pi
    a_seg = jnp.stack([pr.reshape(-1), pi.reshape(-1)])
    return bmat, cmat, a_step, a_seg


def _outproj_kernel(attn_ref, sgu_ref, ssm_ref, h_ref, g_ref, w_ref, out_ref):
    acc = h_ref[0]
    lo = 0
    for part in (attn_ref[0].astype(F32), sgu_ref[0], ssm_ref[0]):
        hi = lo + part.shape[-1]
        acc = acc + _dot((_rms_scale(part) * g_ref[:, lo:hi]).astype(BF16), w_ref[lo:hi, :])
        lo = hi
    out_ref[0] = acc


def _outproj(y_attn, y_sgu, y_ssm, h, g, w):
    B, S, _ = h.shape
    tile = lambda width: pl.BlockSpec((1, ROW_TILE, width), lambda b, s: (b, s, 0))
    return pl.pallas_call(
        _outproj_kernel,
        grid=(B, S // ROW_TILE),
        in_specs=[tile(ATTN_WIDTH), tile(SGU_WIDTH), tile(SSM_WIDTH), tile(D_MODEL),
                  _resident((1, D_MODEL)), _resident((D_MODEL, D_MODEL))],
        out_specs=tile(D_MODEL),
        out_shape=jax.ShapeDtypeStruct(h.shape, F32),
        compiler_params=_params("parallel", "parallel"),
        name="outproj",
    )(y_attn, y_sgu, y_ssm, h, g, w)


def _ffn_kernel(h_ref, p_ref, gffn_ref, wup_ref, cw_ref, cb_ref, wdown_ref, gple_ref, wgate_ref, wproj_ref,
                gfin_ref, out_ref, xn_scr, tail_scr, shift_scr, act_scr, *, final):
    n_chunks = D_FF // FF_CHUNK
    first_tile = pl.program_id(1) == 0
    h = h_ref[0]
    xn_scr[...] = (_rms_scale(h) * gffn_ref[...]).astype(BF16)
    out_ref[0] = h

    def conv(t, col0, slot, buf):
        cols = slice(col0, col0 + FF_CHUNK)
        buf[0:SUBLANES, :] = jnp.where(first_tile, 0.0, tail_scr[slot])
        buf[SUBLANES:, :] = t
        tail_scr[slot] = t[ROW_TILE - SUBLANES:]
        m1 = buf[SUBLANES - 1:SUBLANES - 1 + ROW_TILE, :]
        m2 = buf[SUBLANES - 2:SUBLANES - 2 + ROW_TILE, :]
        return cw_ref[2:3, cols] * t + cw_ref[1:2, cols] * m1 + cw_ref[0:1, cols] * m2 + cb_ref[:, cols]

    for g in range(0, n_chunks, FF_GROUP):
        group = range(g, min(g + FF_GROUP, n_chunks))
        for c in group:
            v0, g0 = c * FF_CHUNK, D_FF + c * FF_CHUNK
            xn = xn_scr[...]
            val = conv(_dot(xn, wup_ref[:, v0:v0 + FF_CHUNK]), v0, 2 * c, shift_scr.at[0])
            gate = conv(_dot(xn, wup_ref[:, g0:g0 + FF_CHUNK]), g0, 2 * c + 1, shift_scr.at[1])
            tanh = jnp.tanh(gate * (GELU_C0 + GELU_C1 * (gate * gate)))
            act_scr[:, (c - g) * FF_CHUNK:(c - g + 1) * FF_CHUNK] = ((gate * val) * (0.5 + 0.5 * tanh)).astype(BF16)
        width = len(group) * FF_CHUNK
        out_ref[0] += _dot(act_scr[:, :width], wdown_ref[g * FF_CHUNK:g * FF_CHUNK + width, :])

    h2 = out_ref[0]
    gate = _sigmoid(_dot((_rms_scale(h2) * gple_ref[...]).astype(BF16), wgate_ref[...]))
    h3 = h2 + gate * _dot(p_ref[...].astype(BF16), wproj_ref[...])
    if final:
        h3 = _rms_scale(h3) * gfin_ref[...]
    out_ref[0] = h3


def _ffn(h, p, layer, g_ffn, w_up, conv_w, conv_b, w_down, g_ple, w_gate, w_proj, g_final, final):
    B, S, _ = h.shape
    tile = lambda width: pl.BlockSpec((1, ROW_TILE, width), lambda b, s: (b, s, 0))
    return pl.pallas_call(
        functools.partial(_ffn_kernel, final=final),
        grid=(B, S // ROW_TILE),
        in_specs=[tile(D_MODEL), pl.BlockSpec((None, None, ROW_TILE, PLE_DIM), lambda b, s: (layer, b, s, 0)),
                  _resident((1, D_MODEL)), _resident((D_MODEL, 2 * D_FF)),
                  _resident((3, 2 * D_FF)), _resident((1, 2 * D_FF)), _resident((D_FF, D_MODEL)),
                  _resident((1, D_MODEL)), _resident((D_MODEL, D_MODEL)), _resident((PLE_DIM, D_MODEL)),
                  _resident((1, D_MODEL))],
        out_specs=tile(D_MODEL),
        out_shape=jax.ShapeDtypeStruct(h.shape, F32),
        scratch_shapes=[pltpu.VMEM((ROW_TILE, D_MODEL), BF16),
                        pltpu.VMEM((2 * (D_FF // FF_CHUNK), SUBLANES, FF_CHUNK), F32),
                        pltpu.VMEM((2, SUBLANES + ROW_TILE, FF_CHUNK), F32),
                        pltpu.VMEM((ROW_TILE, FF_GROUP * FF_CHUNK), BF16)],
        compiler_params=_params("parallel", "arbitrary"),
        name="ffn",
    )(h, p, g_ffn, w_up, conv_w, conv_b, w_down, g_ple, w_gate, w_proj, g_final)


def kernel(x, p, rel_bias, norm_attn_g, w_in, sgu_ln_g, sgu_ln_b, sgu_w, sgu_b, ssm_a_re, ssm_a_im, ssm_log_dt, ssm_b_re, ssm_b_im, ssm_c_re, ssm_c_im, ssm_d, ssm_glu_w, ssm_glu_b, branch_norm_g, w_out, norm_ffn_g, ffn_w_up, ffn_conv_w, ffn_conv_b, ffn_w_down, norm_ple_g, ple_w_gate, ple_w_proj, final_norm_g):
    row = lambda t: t.reshape(1, -1).astype(F32)
    bias = jnp.stack([_bias_table(rel_bias, window, dil) for window, dil in BRANCHES])
    group_avg = jnp.asarray(np.kron(np.eye(SGU_GROUPS), np.full((SGU_GROUP_WIDTH,) * 2, 1.0 / SGU_GROUP_WIDTH)), BF16)
    causal = np.tril(np.ones((SGU_CHUNK, SGU_CHUNK), dtype=bool))

    h = x
    for i in range(DEPTH):
        qkv, z_sgu, z_ssm = _inproj(h, row(norm_attn_g[i]), w_in[i].astype(BF16))

        y_attn = _attention(qkv, bias)

        w_causal = jnp.where(causal, sgu_w[i].astype(F32), 0.0).astype(BF16)
        b_table = jnp.repeat(sgu_b[i].astype(F32).T, SGU_GROUP_WIDTH, axis=1)
        y_sgu = _sgu(z_sgu, group_avg, row(sgu_ln_g[i]), row(sgu_ln_b[i]), w_causal, b_table)

        bmat, cmat, a_step, a_seg = _ssm_matrices(ssm_a_re[i], ssm_a_im[i], ssm_log_dt[i], ssm_b_re[i],
                                                  ssm_b_im[i], ssm_c_re[i], ssm_c_im[i])
        y_ssm = _ssm(z_ssm, bmat, cmat, a_step, a_seg, row(ssm_d[i]), ssm_glu_w[i].astype(BF16),
                     row(ssm_glu_b[i]))

        h = _outproj(y_attn, y_sgu, y_ssm, h, row(branch_norm_g[i]), w_out[i].astype(BF16))

        h = _ffn(h, p, i, row(norm_ffn_g[i]), ffn_w_up[i].astype(BF16), ffn_conv_w[i].astype(F32),
                 row(ffn_conv_b[i]), ffn_w_down[i].astype(BF16), row(norm_ple_g[i]),
                 ple_w_gate[i].astype(BF16), ple_w_proj[i].astype(BF16), row(final_norm_g),
                 final=(i == DEPTH - 1))
    return h
```

```python
import functools
import math

import numpy as np
import jax
import jax.numpy as jnp
from jax import lax
from jax.experimental import pallas as pl
from jax.experimental.pallas import tpu as pltpu

D_MODEL = 1024
DEPTH = 2
PLE_DIM = 256
HEAD_DIM = 64
N_HEADS = 8
ATTN_WIDTH = N_HEADS * HEAD_DIM
BRANCHES = ((128, 1), (512, 4), (2048, 16))
ATTN_BLOCK = 128
N_REL_BUCKETS = 32
REL_MAX_DISTANCE = 2048
SGU_GROUPS = 4
SGU_GROUP_WIDTH = 64
SGU_WIDTH = SGU_GROUPS * SGU_GROUP_WIDTH
SGU_CHUNK = 128
SSM_GROUP_CH = 16
SSM_WIDTH = 256
SSM_GROUPS = SSM_WIDTH // SSM_GROUP_CH
SSM_STATE = 64
SSM_LANES = SSM_GROUPS * SSM_STATE
QKV_WIDTH = 3 * ATTN_WIDTH
IN_WIDTH = QKV_WIDTH + 2 * SGU_WIDTH + SSM_WIDTH
D_FF = 2816
EPS = 1e-6
NEG_INF = -1e30
LOG2E = 1.4426950408889634

LANES = 128
SUBLANES = 8
VMEM_LIMIT = 56 * 1024 * 1024

ROW_TILE = 512
ATTN_SUPER = ATTN_BLOCK * max(d for _, d in BRANCHES)
ATTN_STAGE_DIL = 4
SSM_SEG = 64
SSM_TILE = SUBLANES * SSM_SEG
SSM_PITCH = SSM_SEG + SUBLANES
FF_CHUNK = 256
FF_GROUP = 4
GELU_C0 = 0.7978845608028654
GELU_C1 = GELU_C0 * 0.044715

BF16 = jnp.bfloat16
F32 = jnp.float32


def _gelu(x):
    return 0.5 * x * (1.0 + jnp.tanh(0.7978845608028654 * (x + 0.044715 * (x * x * x))))


def _sigmoid(x):
    return 1.0 / (1.0 + jnp.exp(-x))


def _rms_scale(x):
    return x * lax.rsqrt(jnp.mean(x * x, axis=-1, keepdims=True) + EPS)


def _dot(a, b):
    return jnp.dot(a, b, preferred_element_type=F32)


def _params(*sem):
    return pltpu.CompilerParams(dimension_semantics=sem, vmem_limit_bytes=VMEM_LIMIT)


def _resident(shape):
    nd = len(shape)
    return pl.BlockSpec(shape, lambda *_: (0,) * nd, pipeline_mode=pl.Buffered(1))


def _rows(start, size, stride):
    return pl.ds(start, size) if stride == 1 else pl.ds(start, size, stride=stride)


def _inproj_kernel(x_ref, g_ref, w_ref, qkv_ref, sgu_ref, ssm_ref):
    xn = (_rms_scale(x_ref[0]) * g_ref[...]).astype(BF16)
    for c in range(0, QKV_WIDTH, 512):
        qkv_ref[0, :, c:c + 512] = _dot(xn, w_ref[:, c:c + 512])
    for c in range(0, 2 * SGU_WIDTH, 256):
        sgu_ref[0, :, c:c + 256] = _dot(xn, w_ref[:, QKV_WIDTH + c:QKV_WIDTH + c + 256])
    ssm_ref[0] = _dot(xn, w_ref[:, QKV_WIDTH + 2 * SGU_WIDTH:])


def _inproj(h, g, w):
    B, S, _ = h.shape
    return pl.pallas_call(
        _inproj_kernel,
        grid=(B, S // ROW_TILE),
        in_specs=[pl.BlockSpec((1, ROW_TILE, D_MODEL), lambda b, s: (b, s, 0)),
                  _resident((1, D_MODEL)), _resident((D_MODEL, IN_WIDTH))],
        out_specs=[pl.BlockSpec((1, ROW_TILE, QKV_WIDTH), lambda b, s: (b, s, 0)),
                   pl.BlockSpec((1, ROW_TILE, 2 * SGU_WIDTH), lambda b, s: (b, s, 0)),
                   pl.BlockSpec((1, ROW_TILE, SSM_WIDTH), lambda b, s: (b, s, 0))],
        out_shape=[jax.ShapeDtypeStruct((B, S, QKV_WIDTH), F32),
                   jax.ShapeDtypeStruct((B, S, 2 * SGU_WIDTH), F32),
                   jax.ShapeDtypeStruct((B, S, SSM_WIDTH), F32)],
        compiler_params=_params("parallel", "parallel"),
        name="inproj",
    )(h, g, w)


def _t5_bucket(dist):
    max_exact = N_REL_BUCKETS // 2
    d = np.maximum(dist, 0)
    large = max_exact + (np.log(np.maximum(d, 1) / max_exact)
                         / np.log(REL_MAX_DISTANCE / max_exact)
                         * (N_REL_BUCKETS - max_exact)).astype(np.int32)
    large = np.minimum(large, N_REL_BUCKETS - 1)
    return np.where(d < max_exact, d, large).astype(np.int32)


def _bias_table(rel_bias, window, dil):
    blk = ATTN_BLOCK
    rel = np.arange(blk)[:, None] + blk - np.arange(2 * blk)[None, :]
    band = (rel >= 0) & (rel <= window // dil)
    bucket = np.where(band, _t5_bucket(rel * dil), -1)[None]
    table = jnp.full((N_HEADS, blk, 2 * blk), NEG_INF, F32)
    for n in range(N_REL_BUCKETS):
        if (bucket == n).any():
            table = jnp.where(bucket == n, (rel_bias[n].astype(F32) * LOG2E)[:, None, None], table)
    no_prev = jnp.where(np.arange(2 * blk)[None, None, :] < blk, NEG_INF, table)
    return jnp.stack([table, no_prev])


def _attn_kernel(q_ref, k_ref, v_ref, bias_ref, o_ref, *scr):
    nb = len(BRANCHES)
    k_scr, v_scr, o_scr, l_scr = scr[0:nb], scr[nb:2 * nb], scr[2 * nb:3 * nb], scr[3 * nb:4 * nb]
    blk = ATTN_BLOCK
    first_super = pl.program_id(2) == 0
    hp = pl.program_id(1)

    stage = scr[4 * nb:4 * nb + 2]
    sd = ATTN_STAGE_DIL
    for bi, (_, d) in enumerate(BRANCHES):
        n_d = ATTN_SUPER // d
        for ti, (src, dst) in enumerate(((k_ref, k_scr[bi]), (v_ref, v_scr[bi]))):
            @pl.when(first_super)
            def _(dst=dst):
                dst[:, 0:blk, :] = jnp.zeros((d, blk, LANES), BF16)

            @pl.when(jnp.logical_not(first_super))
            def _(dst=dst):
                dst[:, 0:blk, :] = dst[:, n_d:n_d + blk, :]

            for r in range(d):
                if d > sd and d % sd == 0:
                    x = stage[ti][r % sd, _rows(r // sd, n_d, d // sd), :]
                else:
                    x = src[_rows(r, n_d, d), :]
                    if d == sd:
                        stage[ti][r] = x
                dst[r, blk:, :] = x.astype(BF16)

    lane = lax.broadcasted_iota(jnp.int32, (blk, LANES), 1)
    low_half = lane < HEAD_DIM
    lane2 = lax.broadcasted_iota(jnp.int32, (2 * blk, LANES), 1)
    low2 = lane2 < HEAD_DIM
    zero_kv = jnp.zeros((2 * blk, LANES), BF16)
    ones_low = jnp.where(low2, 1.0, 0.0).astype(BF16)
    ones_high = jnp.where(low2, 0.0, 1.0).astype(BF16)

    for bi, (_, d) in enumerate(BRANCHES):
        q_blocks = ATTN_SUPER // d // blk

        def unit(u, carry, bi=bi, d=d, q_blocks=q_blocks):
            r = u // q_blocks
            qb = u % q_blocks
            tok = _rows(r + qb * (blk * d), blk, d)
            variant = jnp.logical_and(first_super, qb == 0).astype(jnp.int32)
            q = q_ref[tok, :] * (HEAD_DIM ** -0.5 * LOG2E)
            q2 = jnp.concatenate([jnp.where(low_half, q, 0.0), jnp.where(low_half, 0.0, q)], axis=0).astype(BF16)
            win = pl.ds(pl.multiple_of(qb * blk, blk), 2 * blk)
            kk = k_scr[bi][r, win, :]
            vv = v_scr[bi][r, win, :]
            s = lax.dot_general(q2, kk, (((1,), (1,)), ((), ())), preferred_element_type=F32)
            s = s + jnp.concatenate([bias_ref[bi, variant, 2 * hp], bias_ref[bi, variant, 2 * hp + 1]], axis=0)
            m = jnp.max(s, axis=-1, keepdims=True)
            e = jnp.exp2(s - m).astype(BF16)
            e2 = jnp.concatenate([e[:blk], e[blk:]], axis=1)
            v2 = jnp.concatenate(
                [jnp.concatenate([jnp.where(low2, vv, zero_kv), ones_low], axis=1),
                 jnp.concatenate([jnp.where(low2, zero_kv, vv), ones_high], axis=1)], axis=0)
            oe = _dot(e2, v2)
            den = oe[:, LANES:]
            o_scr[bi][tok, :] = oe[:, :LANES] / den
            l_scr[bi][tok, :] = jnp.where(low_half, m[:blk], m[blk:]) + jnp.log2(den)
            return carry

        lax.fori_loop(0, d * q_blocks, unit, 0, unroll=16)

    def mix(c, carry):
        rows = pl.ds(pl.multiple_of(c * 256, 256), 256)
        ls = [l[rows, :] for l in l_scr]
        m = functools.reduce(jnp.maximum, ls)
        es = [jnp.exp2(l - m) for l in ls]
        num = functools.reduce(lambda a, b: a + b, [e * o[rows, :] for e, o in zip(es, o_scr)])
        o_ref[rows, :] = (num / functools.reduce(lambda a, b: a + b, es)).astype(BF16)
        return carry

    lax.fori_loop(0, ATTN_SUPER // 256, mix, 0)


def _attention(qkv, bias):
    B, S, _ = qkv.shape
    pairs = ATTN_WIDTH // LANES
    slab = lambda part: pl.BlockSpec((None, ATTN_SUPER, LANES), lambda b, hp, s: (b, s, part * pairs + hp))
    kv_scr = [pltpu.VMEM((d, ATTN_BLOCK + ATTN_SUPER // d, LANES), BF16) for _, d in BRANCHES]
    tok_scr = [pltpu.VMEM((ATTN_SUPER, LANES), F32) for _ in BRANCHES]
    stage_scr = pltpu.VMEM((ATTN_STAGE_DIL, ATTN_SUPER // ATTN_STAGE_DIL, LANES), F32)
    return pl.pallas_call(
        _attn_kernel,
        grid=(B, pairs, S // ATTN_SUPER),
        in_specs=[slab(0), slab(1), slab(2), _resident(bias.shape)],
        out_specs=pl.BlockSpec((None, ATTN_SUPER, LANES), lambda b, hp, s: (b, s, hp)),
        out_shape=jax.ShapeDtypeStruct((B, S, ATTN_WIDTH), BF16),
        scratch_shapes=kv_scr + kv_scr + tok_scr + tok_scr + [stage_scr, stage_scr],
        compiler_params=_params("parallel", "parallel", "arbitrary"),
        name="attn",
    )(qkv, qkv, qkv, bias)


def _sgu_kernel(z_ref, avg_ref, lng_ref, lnb_ref, w_ref, bt_ref, y_ref):
    z = _gelu(z_ref[0])
    u = z[:, :SGU_WIDTH]
    v = z[:, SGU_WIDTH:]
    avg = avg_ref[...]

    def group_mean(t):
        hi = t.astype(BF16)
        lo = (t - hi.astype(F32)).astype(BF16)
        return _dot(hi, avg) + _dot(lo, avg)

    d = v - group_mean(v)
    vn = d * lax.rsqrt(group_mean(d * d) + EPS) * lng_ref[...] + lnb_ref[...]
    lane = lax.broadcasted_iota(jnp.int32, (SGU_CHUNK, LANES), 1)
    low_half = lane < SGU_GROUP_WIDTH
    for c in range(ROW_TILE // SGU_CHUNK):
        rows = slice(c * SGU_CHUNK, (c + 1) * SGU_CHUNK)
        for p in range(SGU_GROUPS // 2):
            cols = slice(p * LANES, (p + 1) * LANES)
            vp = vn[rows, cols].astype(BF16)
            mixed = jnp.where(low_half, _dot(w_ref[2 * p], vp), _dot(w_ref[2 * p + 1], vp))
            y_ref[0, rows, cols] = u[rows, cols] * (mixed + bt_ref[:, cols])


def _sgu(z, avg, ln_g, ln_b, w_causal, b_table):
    B, S, _ = z.shape
    return pl.pallas_call(
        _sgu_kernel,
        grid=(B, S // ROW_TILE),
        in_specs=[pl.BlockSpec((1, ROW_TILE, 2 * SGU_WIDTH), lambda b, s: (b, s, 0)),
                  _resident((SGU_WIDTH, SGU_WIDTH)), _resident((1, SGU_WIDTH)), _resident((1, SGU_WIDTH)),
                  _resident((SGU_GROUPS, SGU_CHUNK, SGU_CHUNK)), _resident((SGU_CHUNK, SGU_WIDTH))],
        out_specs=pl.BlockSpec((1, ROW_TILE, SGU_WIDTH), lambda b, s: (b, s, 0)),
        out_shape=jax.ShapeDtypeStruct((B, S, SGU_WIDTH), F32),
        compiler_params=_params("parallel", "parallel"),
        name="sgu",
    )(z, avg, ln_g, ln_b, w_causal, b_table)


def _ssm_kernel(u_ref, bmat_ref, cmat_ref, a_ref, aseg_ref, d_ref, gw_ref, gb_ref, y_ref,
                perm_scr, x_scr, carry_scr, xin_scr, xb_scr):
    N = SSM_LANES
    slabs = SSM_WIDTH // LANES

    @pl.when(pl.program_id(1) == 0)
    def _():
        carry_scr[...] = jnp.zeros_like(carry_scr)

    for j in range(SUBLANES):
        for c in range(slabs):
            perm_scr[c, j * SSM_PITCH:j * SSM_PITCH + SSM_SEG, :] = \
                u_ref[j * SSM_SEG:(j + 1) * SSM_SEG, c * LANES:(c + 1) * LANES]
    u = jnp.concatenate(
        [jnp.concatenate([perm_scr[c, pl.ds(t, SUBLANES, stride=SSM_PITCH), :] for c in range(slabs)], axis=1)
         for t in range(SSM_SEG)], axis=0)
    x_scr[...] = _dot(u.astype(BF16), bmat_ref[...])

    a_re = jnp.broadcast_to(a_ref[0:1, :], (SUBLANES, N))
    a_im = jnp.broadcast_to(a_ref[1:2, :], (SUBLANES, N))

    def local_step(t, x):
        xr, xi = x
        rows = pl.ds(pl.multiple_of(t * SUBLANES, SUBLANES), SUBLANES)
        nr = a_re * xr - a_im * xi + x_scr[rows, 0:N]
        ni = a_re * xi + a_im * xr + x_scr[rows, N:2 * N]
        x_scr[rows, 0:N] = nr
        x_scr[rows, N:2 * N] = ni
        return nr, ni

    zeros = jnp.zeros((SUBLANES, N), F32)
    end_re, end_im = lax.fori_loop(0, SSM_SEG, local_step, (zeros, zeros), unroll=2)

    s_re, s_im = aseg_ref[0:1, :], aseg_ref[1:2, :]
    cr, ci = carry_scr[0:1, :], carry_scr[1:2, :]
    for j in range(SUBLANES):
        xin_scr[j:j + 1, 0:N] = cr
        xin_scr[j:j + 1, N:2 * N] = ci
        er, ei = end_re[j:j + 1, :], end_im[j:j + 1, :]
        cr, ci = s_re * cr - s_im * ci + er, s_re * ci + s_im * cr + ei
    carry_scr[0:1, :] = cr
    carry_scr[1:2, :] = ci

    def fix_step(i, f):
        fr, fi = f
        out_re, out_im = [], []
        for k in range(2):
            fr, fi = a_re * fr - a_im * fi, a_re * fi + a_im * fr
            rows = pl.ds(pl.multiple_of(i * (2 * SUBLANES) + k * SUBLANES, SUBLANES), SUBLANES)
            out_re.append(x_scr[rows, 0:N] + fr)
            out_im.append(x_scr[rows, N:2 * N] + fi)
        rows2 = pl.ds(pl.multiple_of(i * (2 * SUBLANES), 2 * SUBLANES), 2 * SUBLANES)
        xb_scr[rows2, 0:N] = jnp.concatenate(out_re, axis=0).astype(BF16)
        xb_scr[rows2, N:2 * N] = jnp.concatenate(out_im, axis=0).astype(BF16)
        return fr, fi

    lax.fori_loop(0, SSM_SEG // 2, fix_step, (xin_scr[:, 0:N], xin_scr[:, N:2 * N]))

    y = _dot(xb_scr[...], cmat_ref[...]) + d_ref[...] * u
    y = _gelu(y)
    y = y * _sigmoid(_dot(y.astype(BF16), gw_ref[...]) + gb_ref[...])
    for t in range(SSM_SEG):
        for c in range(slabs):
            perm_scr[c, pl.ds(t, SUBLANES, stride=SSM_PITCH), :] = \
                y[t * SUBLANES:(t + 1) * SUBLANES, c * LANES:(c + 1) * LANES]
    for j in range(SUBLANES):
        for c in range(slabs):
            y_ref[j * SSM_SEG:(j + 1) * SSM_SEG, c * LANES:(c + 1) * LANES] = \
                perm_scr[c, j * SSM_PITCH:j * SSM_PITCH + SSM_SEG, :]


def _ssm(u, bmat, cmat, a_step, a_seg, d_skip, glu_w, glu_b):
    B, S, _ = u.shape
    blk = pl.BlockSpec((None, SSM_TILE, SSM_WIDTH), lambda b, s: (b, s, 0))
    return pl.pallas_call(
        _ssm_kernel,
        grid=(B, S // SSM_TILE),
        in_specs=[blk, _resident((SSM_WIDTH, 2 * SSM_LANES)), _resident((2 * SSM_LANES, SSM_WIDTH)),
                  _resident((2, SSM_LANES)), _resident((2, SSM_LANES)), _resident((1, SSM_WIDTH)),
                  _resident((SSM_WIDTH, SSM_WIDTH)), _resident((1, SSM_WIDTH))],
        out_specs=blk,
        out_shape=jax.ShapeDtypeStruct(u.shape, F32),
        scratch_shapes=[pltpu.VMEM((SSM_WIDTH // LANES, SUBLANES * SSM_PITCH, LANES), F32),
                        pltpu.VMEM((SSM_TILE, 2 * SSM_LANES), F32), pltpu.VMEM((2, SSM_LANES), F32),
                        pltpu.VMEM((SUBLANES, 2 * SSM_LANES), F32),
                        pltpu.VMEM((SSM_TILE, 2 * SSM_LANES), BF16)],
        compiler_params=_params("parallel", "arbitrary"),
        name="ssm",
    )(u, bmat, cmat, a_step, a_seg, d_skip, glu_w, glu_b)


def _ssm_matrices(a_re, a_im, log_dt, b_re, b_im, c_re, c_im):
    G, N, C = SSM_GROUPS, SSM_STATE, SSM_GROUP_CH
    dt = jnp.exp(log_dt)[:, None]
    mag = jnp.exp(a_re * dt)
    ab_re, ab_im = mag * jnp.cos(a_im * dt), mag * jnp.sin(a_im * dt)
    den = a_re * a_re + a_im * a_im
    f_re = ((ab_re - 1.0) * a_re + ab_im * a_im) / den
    f_im = (ab_im * a_re - (ab_re - 1.0) * a_im) / den
    bb_re = f_re[:, :, None] * b_re - f_im[:, :, None] * b_im
    bb_im = f_re[:, :, None] * b_im + f_im[:, :, None] * b_re
    eye = jnp.eye(G, dtype=F32)
    bd_in = lambda t: jnp.einsum('gnc,gh->gchn', t, eye).reshape(G * C, G * N)
    bd_out = lambda t: jnp.einsum('gcn,gh->gnhc', t, eye).reshape(G * N, G * C)
    bmat = jnp.concatenate([bd_in(bb_re), bd_in(bb_im)], axis=1).astype(BF16)
    cmat = jnp.concatenate([bd_out(c_re), -bd_out(c_im)], axis=0).astype(BF16)
    a_step = jnp.stack([ab_re.reshape(-1), ab_im.reshape(-1)])
    pr, pi = ab_re, ab_im
    for _ in range(int(math.log2(SSM_SEG))):
        pr, pi = pr * pr - pi * pi, 2.0 * pr * pi
    a_seg = jnp.stack([pr.reshape(-1), pi.reshape(-1)])
    return bmat, cmat, a_step, a_seg


def _ffn_kernel(attn_ref, sgu_ref, ssm_ref, h_ref, p_ref, gmix_ref, wout_ref, gffn_ref, wup_ref, cw_ref, cb_ref,
                wdown_ref, gple_ref, wgate_ref, wproj_ref, gfin_ref, out_ref, xn_scr, tail_scr, shift_scr, act_scr,
                *, final):
    n_chunks = D_FF // FF_CHUNK
    first_tile = pl.program_id(1) == 0

    h = h_ref[0]
    lo = 0
    for part in (attn_ref[0].astype(F32), sgu_ref[0], ssm_ref[0]):
        hi = lo + part.shape[-1]
        h = h + _dot((_rms_scale(part) * gmix_ref[:, lo:hi]).astype(BF16), wout_ref[lo:hi, :])
        lo = hi

    xn_scr[...] = (_rms_scale(h) * gffn_ref[...]).astype(BF16)
    out_ref[0] = h

    def conv(t, col0, slot, buf):
        cols = slice(col0, col0 + FF_CHUNK)
        buf[0:SUBLANES, :] = jnp.where(first_tile, 0.0, tail_scr[slot])
        buf[SUBLANES:, :] = t
        tail_scr[slot] = t[ROW_TILE - SUBLANES:]
        m1 = buf[SUBLANES - 1:SUBLANES - 1 + ROW_TILE, :]
        m2 = buf[SUBLANES - 2:SUBLANES - 2 + ROW_TILE, :]
        return cw_ref[2:3, cols] * t + cw_ref[1:2, cols] * m1 + cw_ref[0:1, cols] * m2 + cb_ref[:, cols]

    for g in range(0, n_chunks, FF_GROUP):
        group = range(g, min(g + FF_GROUP, n_chunks))
        for c in group:
            v0, g0 = c * FF_CHUNK, D_FF + c * FF_CHUNK
            xn = xn_scr[...]
            val = conv(_dot(xn, wup_ref[:, v0:v0 + FF_CHUNK]), v0, 2 * c, shift_scr.at[0])
            gate = conv(_dot(xn, wup_ref[:, g0:g0 + FF_CHUNK]), g0, 2 * c + 1, shift_scr.at[1])
            tanh = jnp.tanh(gate * (GELU_C0 + GELU_C1 * (gate * gate)))
            act_scr[:, (c - g) * FF_CHUNK:(c - g + 1) * FF_CHUNK] = ((gate * val) * (0.5 + 0.5 * tanh)).astype(BF16)
        width = len(group) * FF_CHUNK
        out_ref[0] += _dot(act_scr[:, :width], wdown_ref[g * FF_CHUNK:g * FF_CHUNK + width, :])

    h2 = out_ref[0]
    gate = _sigmoid(_dot((_rms_scale(h2) * gple_ref[...]).astype(BF16), wgate_ref[...]))
    h3 = h2 + gate * _dot(p_ref[...].astype(BF16), wproj_ref[...])
    if final:
        h3 = _rms_scale(h3) * gfin_ref[...]
    out_ref[0] = h3


def _ffn(y_attn, y_sgu, y_ssm, h, p, layer, g_mix, w_out, g_ffn, w_up, conv_w, conv_b, w_down, g_ple, w_gate,
         w_proj, g_final, final):
    B, S, _ = h.shape
    tile = lambda width: pl.BlockSpec((1, ROW_TILE, width), lambda b, s: (b, s, 0))
    return pl.pallas_call(
        functools.partial(_ffn_kernel, final=final),
        grid=(B, S // ROW_TILE),
        in_specs=[tile(ATTN_WIDTH), tile(SGU_WIDTH), tile(SSM_WIDTH), tile(D_MODEL),
                  pl.BlockSpec((None, None, ROW_TILE, PLE_DIM), lambda b, s: (layer, b, s, 0)),
                  _resident((1, D_MODEL)), _resident((D_MODEL, D_MODEL)),
                  _resident((1, D_MODEL)), _resident((D_MODEL, 2 * D_FF)),
                  _resident((3, 2 * D_FF)), _resident((1, 2 * D_FF)), _resident((D_FF, D_MODEL)),
                  _resident((1, D_MODEL)), _resident((D_MODEL, D_MODEL)), _resident((PLE_DIM, D_MODEL)),
                  _resident((1, D_MODEL))],
        out_specs=tile(D_MODEL),
        out_shape=jax.ShapeDtypeStruct(h.shape, F32),
        scratch_shapes=[pltpu.VMEM((ROW_TILE, D_MODEL), BF16),
                        pltpu.VMEM((2 * (D_FF // FF_CHUNK), SUBLANES, FF_CHUNK), F32),
                        pltpu.VMEM((2, SUBLANES + ROW_TILE, FF_CHUNK), F32),
                        pltpu.VMEM((ROW_TILE, FF_GROUP * FF_CHUNK), BF16)],
        compiler_params=_params("parallel", "arbitrary"),
        name="ffn",
    )(y_attn, y_sgu, y_ssm, h, p, g_mix, w_out, g_ffn, w_up, conv_w, conv_b, w_down, g_ple, w_gate, w_proj, g_final)


def kernel(x, p, rel_bias, norm_attn_g, w_in, sgu_ln_g, sgu_ln_b, sgu_w, sgu_b, ssm_a_re, ssm_a_im, ssm_log_dt, ssm_b_re, ssm_b_im, ssm_c_re, ssm_c_im, ssm_d, ssm_glu_w, ssm_glu_b, branch_norm_g, w_out, norm_ffn_g, ffn_w_up, ffn_conv_w, ffn_conv_b, ffn_w_down, norm_ple_g, ple_w_gate, ple_w_proj, final_norm_g):
    row = lambda t: t.reshape(1, -1).astype(F32)
    bias = jnp.stack([_bias_table(rel_bias, window, dil) for window, dil in BRANCHES])
    group_avg = jnp.asarray(np.kron(np.eye(SGU_GROUPS), np.full((SGU_GROUP_WIDTH,) * 2, 1.0 / SGU_GROUP_WIDTH)), BF16)
    causal = np.tril(np.ones((SGU_CHUNK, SGU_CHUNK), dtype=bool))

    h = x
    for i in range(DEPTH):
        qkv, z_sgu, z_ssm = _inproj(h, row(norm_attn_g[i]), w_in[i].astype(BF16))

        y_attn = _attention(qkv, bias)

        w_causal = jnp.where(causal, sgu_w[i].astype(F32), 0.0).astype(BF16)
        b_table = jnp.repeat(sgu_b[i].astype(F32).T, SGU_GROUP_WIDTH, axis=1)
        y_sgu = _sgu(z_sgu, group_avg, row(sgu_ln_g[i]), row(sgu_ln_b[i]), w_causal, b_table)

        bmat, cmat, a_step, a_seg = _ssm_matrices(ssm_a_re[i], ssm_a_im[i], ssm_log_dt[i], ssm_b_re[i],
                                                  ssm_b_im[i], ssm_c_re[i], ssm_c_im[i])
        y_ssm = _ssm(z_ssm, bmat, cmat, a_step, a_seg, row(ssm_d[i]), ssm_glu_w[i].astype(BF16),
                     row(ssm_glu_b[i]))

        h = _ffn(y_attn, y_sgu, y_ssm, h, p, i, row(branch_norm_g[i]), w_out[i].astype(BF16),
                 row(norm_ffn_g[i]), ffn_w_up[i].astype(BF16), ffn_conv_w[i].astype(F32),
                 row(ffn_conv_b[i]), ffn_w_down[i].astype(BF16), row(norm_ple_g[i]),
                 ple_w_gate[i].astype(BF16), ple_w_proj[i].astype(BF16), row(final_norm_g),
                 final=(i == DEPTH - 1))
    return h
```

```python
import functools
import math

import numpy as np
import jax
import jax.numpy as jnp
from jax import lax
from jax.experimental import pallas as pl
from jax.experimental.pallas import tpu as pltpu

D_MODEL = 1024
DEPTH = 2
PLE_DIM = 256
HEAD_DIM = 64
N_HEADS = 8
ATTN_WIDTH = N_HEADS * HEAD_DIM
BRANCHES = ((128, 1), (512, 4), (2048, 16))
ATTN_BLOCK = 128
N_REL_BUCKETS = 32
REL_MAX_DISTANCE = 2048
SGU_GROUPS = 4
SGU_GROUP_WIDTH = 64
SGU_WIDTH = SGU_GROUPS * SGU_GROUP_WIDTH
SGU_CHUNK = 128
SSM_GROUP_CH = 16
SSM_WIDTH = 256
SSM_GROUPS = SSM_WIDTH // SSM_GROUP_CH
SSM_STATE = 64
SSM_LANES = SSM_GROUPS * SSM_STATE
QKV_WIDTH = 3 * ATTN_WIDTH
IN_WIDTH = QKV_WIDTH + 2 * SGU_WIDTH + SSM_WIDTH
D_FF = 2816
EPS = 1e-6
NEG_INF = -1e30
LOG2E = 1.4426950408889634

LANES = 128
SUBLANES = 8
VMEM_LIMIT = 56 * 1024 * 1024

ROW_TILE = 512
ATTN_SUPER = ATTN_BLOCK * max(d for _, d in BRANCHES)
ATTN_STAGE_DIL = 4
SSM_SEG = 64
SSM_TILE = SUBLANES * SSM_SEG
SSM_PITCH = SSM_SEG + SUBLANES
FF_CHUNK = 256
FF_GROUP = 4
GELU_C0 = 0.7978845608028654
GELU_C1 = GELU_C0 * 0.044715

BF16 = jnp.bfloat16
F32 = jnp.float32


def _gelu(x):
    return 0.5 * x * (1.0 + jnp.tanh(0.7978845608028654 * (x + 0.044715 * (x * x * x))))


def _sigmoid(x):
    return 1.0 / (1.0 + jnp.exp(-x))


def _rms_scale(x):
    return x * lax.rsqrt(jnp.mean(x * x, axis=-1, keepdims=True) + EPS)


def _dot(a, b):
    return jnp.dot(a, b, preferred_element_type=F32)


def _params(*sem):
    return pltpu.CompilerParams(dimension_semantics=sem, vmem_limit_bytes=VMEM_LIMIT)


def _resident(shape):
    nd = len(shape)
    return pl.BlockSpec(shape, lambda *_: (0,) * nd, pipeline_mode=pl.Buffered(1))


def _rows(start, size, stride):
    return pl.ds(start, size) if stride == 1 else pl.ds(start, size, stride=stride)


def _mixin_kernel(x_ref, g_ref, w_ref, avg_ref, lng_ref, lnb_ref, wsgu_ref, bt_ref, bmat_ref, cmat_ref, a_ref,
                  aseg_ref, d_ref, gw_ref, gb_ref, qkv_ref, ysgu_ref, yssm_ref, xn_scr, *ssm_scr):
    xn_scr[...] = (_rms_scale(x_ref[0]) * g_ref[...]).astype(BF16)
    o_s = QKV_WIDTH + 2 * SGU_WIDTH
    _ssm_tile(_dot(xn_scr[...], w_ref[:, o_s:]), pl.program_id(1) == 0, bmat_ref, cmat_ref, a_ref, aseg_ref,
              d_ref, gw_ref, gb_ref, yssm_ref, *ssm_scr)
    z_sgu = jnp.concatenate([_dot(xn_scr[...], w_ref[:, QKV_WIDTH + c:QKV_WIDTH + c + 256])
                             for c in range(0, 2 * SGU_WIDTH, 256)], axis=1)
    _sgu_tile(z_sgu, avg_ref, lng_ref, lnb_ref, wsgu_ref, bt_ref, ysgu_ref)
    for c in range(0, QKV_WIDTH, 512):
        qkv_ref[0, :, c:c + 512] = _dot(xn_scr[...], w_ref[:, c:c + 512])


def _mixin(h, g, w, sgu_args, ssm_args):
    B, S, _ = h.shape
    assert SSM_TILE == ROW_TILE and ROW_TILE % SGU_CHUNK == 0
    tile = lambda width: pl.BlockSpec((1, ROW_TILE, width), lambda b, s: (b, s, 0))
    return pl.pallas_call(
        _mixin_kernel,
        grid=(B, S // ROW_TILE),
        in_specs=[tile(D_MODEL), _resident((1, D_MODEL)), _resident((D_MODEL, IN_WIDTH))]
                 + [_resident(t.shape) for t in sgu_args] + [_resident(t.shape) for t in ssm_args],
        out_specs=[tile(QKV_WIDTH), tile(SGU_WIDTH), tile(SSM_WIDTH)],
        out_shape=[jax.ShapeDtypeStruct((B, S, QKV_WIDTH), F32),
                   jax.ShapeDtypeStruct((B, S, SGU_WIDTH), F32),
                   jax.ShapeDtypeStruct((B, S, SSM_WIDTH), F32)],
        scratch_shapes=[pltpu.VMEM((ROW_TILE, D_MODEL), BF16),
                        pltpu.VMEM((SSM_WIDTH // LANES, SUBLANES * SSM_PITCH, LANES), F32),
                        pltpu.VMEM((SSM_TILE, 2 * SSM_LANES), F32), pltpu.VMEM((2, SSM_LANES), F32),
                        pltpu.VMEM((SUBLANES, 2 * SSM_LANES), F32),
                        pltpu.VMEM((SSM_TILE, 2 * SSM_LANES), BF16)],
        compiler_params=_params("parallel", "arbitrary"),
        name="mixin",
    )(h, g, w, *sgu_args, *ssm_args)


def _t5_bucket(dist):
    max_exact = N_REL_BUCKETS // 2
    d = np.maximum(dist, 0)
    large = max_exact + (np.log(np.maximum(d, 1) / max_exact)
                         / np.log(REL_MAX_DISTANCE / max_exact)
                         * (N_REL_BUCKETS - max_exact)).astype(np.int32)
    large = np.minimum(large, N_REL_BUCKETS - 1)
    return np.where(d < max_exact, d, large).astype(np.int32)


def _bias_table(rel_bias, window, dil):
    blk = ATTN_BLOCK
    rel = np.arange(blk)[:, None] + blk - np.arange(2 * blk)[None, :]
    band = (rel >= 0) & (rel <= window // dil)
    bucket = np.where(band, _t5_bucket(rel * dil), -1)[None]
    table = jnp.full((N_HEADS, blk, 2 * blk), NEG_INF, F32)
    for n in range(N_REL_BUCKETS):
        if (bucket == n).any():
            table = jnp.where(bucket == n, (rel_bias[n].astype(F32) * LOG2E)[:, None, None], table)
    no_prev = jnp.where(np.arange(2 * blk)[None, None, :] < blk, NEG_INF, table)
    return jnp.stack([table, no_prev])


def _attn_kernel(q_ref, k_ref, v_ref, bias_ref, o_ref, *scr):
    nb = len(BRANCHES)
    k_scr, v_scr, o_scr, l_scr = scr[0:nb], scr[nb:2 * nb], scr[2 * nb:3 * nb], scr[3 * nb:4 * nb]
    blk = ATTN_BLOCK
    first_super = pl.program_id(2) == 0
    hp = pl.program_id(1)

    stage = scr[4 * nb:4 * nb + 2]
    sd = ATTN_STAGE_DIL
    for bi, (_, d) in enumerate(BRANCHES):
        n_d = ATTN_SUPER // d
        for ti, (src, dst) in enumerate(((k_ref, k_scr[bi]), (v_ref, v_scr[bi]))):
            @pl.when(first_super)
            def _(dst=dst):
                dst[:, 0:blk, :] = jnp.zeros((d, blk, LANES), BF16)

            @pl.when(jnp.logical_not(first_super))
            def _(dst=dst):
                dst[:, 0:blk, :] = dst[:, n_d:n_d + blk, :]

            for r in range(d):
                if d > sd and d % sd == 0:
                    x = stage[ti][r % sd, _rows(r // sd, n_d, d // sd), :]
                else:
                    x = src[_rows(r, n_d, d), :]
                    if d == sd:
                        stage[ti][r] = x
                dst[r, blk:, :] = x.astype(BF16)

    lane = lax.broadcasted_iota(jnp.int32, (blk, LANES), 1)
    low_half = lane < HEAD_DIM
    lane2 = lax.broadcasted_iota(jnp.int32, (2 * blk, LANES), 1)
    low2 = lane2 < HEAD_DIM
    zero_kv = jnp.zeros((2 * blk, LANES), BF16)
    ones_low = jnp.where(low2, 1.0, 0.0).astype(BF16)
    ones_high = jnp.where(low2, 0.0, 1.0).astype(BF16)

    for bi, (_, d) in enumerate(BRANCHES):
        q_blocks = ATTN_SUPER // d // blk

        def unit(u, carry, bi=bi, d=d, q_blocks=q_blocks):
            r = u // q_blocks
            qb = u % q_blocks
            tok = _rows(r + qb * (blk * d), blk, d)
            variant = jnp.logical_and(first_super, qb == 0).astype(jnp.int32)
            q = q_ref[tok, :] * (HEAD_DIM ** -0.5 * LOG2E)
            q2 = jnp.concatenate([jnp.where(low_half, q, 0.0), jnp.where(low_half, 0.0, q)], axis=0).astype(BF16)
            win = pl.ds(pl.multiple_of(qb * blk, blk), 2 * blk)
            kk = k_scr[bi][r, win, :]
            vv = v_scr[bi][r, win, :]
            s = lax.dot_general(q2, kk, (((1,), (1,)), ((), ())), preferred_element_type=F32)
            s = s + jnp.concatenate([bias_ref[bi, variant, 2 * hp], bias_ref[bi, variant, 2 * hp + 1]], axis=0)
            m = jnp.max(s, axis=-1, keepdims=True)
            e = jnp.exp2(s - m).astype(BF16)
            e2 = jnp.concatenate([e[:blk], e[blk:]], axis=1)
            v2 = jnp.concatenate(
                [jnp.concatenate([jnp.where(low2, vv, zero_kv), ones_low], axis=1),
                 jnp.concatenate([jnp.where(low2, zero_kv, vv), ones_high], axis=1)], axis=0)
            oe = _dot(e2, v2)
            den = oe[:, LANES:]
            o_scr[bi][tok, :] = oe[:, :LANES] / den
            l_scr[bi][tok, :] = jnp.where(low_half, m[:blk], m[blk:]) + jnp.log2(den)
            return carry

        lax.fori_loop(0, d * q_blocks, unit, 0, unroll=16)

    def mix(c, carry):
        rows = pl.ds(pl.multiple_of(c * 256, 256), 256)
        ls = [l[rows, :] for l in l_scr]
        m = functools.reduce(jnp.maximum, ls)
        es = [jnp.exp2(l - m) for l in ls]
        num = functools.reduce(lambda a, b: a + b, [e * o[rows, :] for e, o in zip(es, o_scr)])
        o_ref[rows, :] = (num / functools.reduce(lambda a, b: a + b, es)).astype(BF16)
        return carry

    lax.fori_loop(0, ATTN_SUPER // 256, mix, 0)


def _attention(qkv, bias):
    B, S, _ = qkv.shape
    pairs = ATTN_WIDTH // LANES
    slab = lambda part: pl.BlockSpec((None, ATTN_SUPER, LANES), lambda b, hp, s: (b, s, part * pairs + hp))
    kv_scr = [pltpu.VMEM((d, ATTN_BLOCK + ATTN_SUPER // d, LANES), BF16) for _, d in BRANCHES]
    tok_scr = [pltpu.VMEM((ATTN_SUPER, LANES), F32) for _ in BRANCHES]
    stage_scr = pltpu.VMEM((ATTN_STAGE_DIL, ATTN_SUPER // ATTN_STAGE_DIL, LANES), F32)
    return pl.pallas_call(
        _attn_kernel,
        grid=(B, pairs, S // ATTN_SUPER),
        in_specs=[slab(0), slab(1), slab(2), _resident(bias.shape)],
        out_specs=pl.BlockSpec((None, ATTN_SUPER, LANES), lambda b, hp, s: (b, s, hp)),
        out_shape=jax.ShapeDtypeStruct((B, S, ATTN_WIDTH), BF16),
        scratch_shapes=kv_scr + kv_scr + tok_scr + tok_scr + [stage_scr, stage_scr],
        compiler_params=_params("parallel", "parallel", "arbitrary"),
        name="attn",
    )(qkv, qkv, qkv, bias)


def _sgu_tile(z, avg_ref, lng_ref, lnb_ref, w_ref, bt_ref, y_ref):
    z = _gelu(z)
    u = z[:, :SGU_WIDTH]
    v = z[:, SGU_WIDTH:]
    avg = avg_ref[...]

    def group_mean(t):
        hi = t.astype(BF16)
        lo = (t - hi.astype(F32)).astype(BF16)
        return _dot(hi, avg) + _dot(lo, avg)

    d = v - group_mean(v)
    vn = d * lax.rsqrt(group_mean(d * d) + EPS) * lng_ref[...] + lnb_ref[...]
    lane = lax.broadcasted_iota(jnp.int32, (SGU_CHUNK, LANES), 1)
    low_half = lane < SGU_GROUP_WIDTH
    for c in range(ROW_TILE // SGU_CHUNK):
        rows = slice(c * SGU_CHUNK, (c + 1) * SGU_CHUNK)
        for p in range(SGU_GROUPS // 2):
            cols = slice(p * LANES, (p + 1) * LANES)
            vp = vn[rows, cols].astype(BF16)
            mixed = jnp.where(low_half, _dot(w_ref[2 * p], vp), _dot(w_ref[2 * p + 1], vp))
            y_ref[0, rows, cols] = u[rows, cols] * (mixed + bt_ref[:, cols])


def _ssm_tile(z, first_tile, bmat_ref, cmat_ref, a_ref, aseg_ref, d_ref, gw_ref, gb_ref, y_ref,
              perm_scr, x_scr, carry_scr, xin_scr, xb_scr):
    N = SSM_LANES
    slabs = SSM_WIDTH // LANES

    @pl.when(first_tile)
    def _():
        carry_scr[...] = jnp.zeros_like(carry_scr)

    for j in range(SUBLANES):
        for c in range(slabs):
            perm_scr[c, j * SSM_PITCH:j * SSM_PITCH + SSM_SEG, :] = \
                z[j * SSM_SEG:(j + 1) * SSM_SEG, c * LANES:(c + 1) * LANES]
    u = jnp.concatenate(
        [jnp.concatenate([perm_scr[c, pl.ds(t, SUBLANES, stride=SSM_PITCH), :] for c in range(slabs)], axis=1)
         for t in range(SSM_SEG)], axis=0)
    x_scr[...] = _dot(u.astype(BF16), bmat_ref[...])

    a_re = jnp.broadcast_to(a_ref[0:1, :], (SUBLANES, N))
    a_im = jnp.broadcast_to(a_ref[1:2, :], (SUBLANES, N))

    end_re = end_im = jnp.zeros((SUBLANES, N), F32)
    for t in range(SSM_SEG):
        rows = slice(t * SUBLANES, (t + 1) * SUBLANES)
        end_re, end_im = (a_re * end_re - a_im * end_im + x_scr[rows, 0:N],
                          a_re * end_im + a_im * end_re + x_scr[rows, N:2 * N])
        x_scr[rows, 0:N] = end_re
        x_scr[rows, N:2 * N] = end_im

    s_re, s_im = aseg_ref[0:1, :], aseg_ref[1:2, :]
    cr, ci = carry_scr[0:1, :], carry_scr[1:2, :]
    for j in range(SUBLANES):
        xin_scr[j:j + 1, 0:N] = cr
        xin_scr[j:j + 1, N:2 * N] = ci
        er, ei = end_re[j:j + 1, :], end_im[j:j + 1, :]
        cr, ci = s_re * cr - s_im * ci + er, s_re * ci + s_im * cr + ei
    carry_scr[0:1, :] = cr
    carry_scr[1:2, :] = ci

    fr, fi = xin_scr[:, 0:N], xin_scr[:, N:2 * N]
    for i in range(SSM_SEG // 2):
        out_re, out_im = [], []
        for k in range(2):
            fr, fi = a_re * fr - a_im * fi, a_re * fi + a_im * fr
            rows = slice((2 * i + k) * SUBLANES, (2 * i + k + 1) * SUBLANES)
            out_re.append(x_scr[rows, 0:N] + fr)
            out_im.append(x_scr[rows, N:2 * N] + fi)
        rows2 = slice(2 * i * SUBLANES, (2 * i + 2) * SUBLANES)
        xb_scr[rows2, 0:N] = jnp.concatenate(out_re, axis=0).astype(BF16)
        xb_scr[rows2, N:2 * N] = jnp.concatenate(out_im, axis=0).astype(BF16)

    y = _dot(xb_scr[...], cmat_ref[...]) + d_ref[...] * u
    y = _gelu(y)
    y = y * _sigmoid(_dot(y.astype(BF16), gw_ref[...]) + gb_ref[...])
    for t in range(SSM_SEG):
        for c in range(slabs):
            perm_scr[c, pl.ds(t, SUBLANES, stride=SSM_PITCH), :] = \
                y[t * SUBLANES:(t + 1) * SUBLANES, c * LANES:(c + 1) * LANES]
    for j in range(SUBLANES):
        for c in range(slabs):
            y_ref[0, j * SSM_SEG:(j + 1) * SSM_SEG, c * LANES:(c + 1) * LANES] = \
                perm_scr[c, j * SSM_PITCH:j * SSM_PITCH + SSM_SEG, :]


def _ssm_matrices(a_re, a_im, log_dt, b_re, b_im, c_re, c_im):
    G, N, C = SSM_GROUPS, SSM_STATE, SSM_GROUP_CH
    dt = jnp.exp(log_dt)[:, None]
    mag = jnp.exp(a_re * dt)
    ab_re, ab_im = mag * jnp.cos(a_im * dt), mag * jnp.sin(a_im * dt)
    den = a_re * a_re + a_im * a_im
    f_re = ((ab_re - 1.0) * a_re + ab_im * a_im) / den
    f_im = (ab_im * a_re - (ab_re - 1.0) * a_im) / den
    bb_re = f_re[:, :, None] * b_re - f_im[:, :, None] * b_im
    bb_im = f_re[:, :, None] * b_im + f_im[:, :, None] * b_re
    eye = jnp.eye(G, dtype=F32)
    bd_in = lambda t: jnp.einsum('gnc,gh->gchn', t, eye).reshape(G * C, G * N)
    bd_out = lambda t: jnp.einsum('gcn,gh->gnhc', t, eye).reshape(G * N, G * C)
    bmat = jnp.concatenate([bd_in(bb_re), bd_in(bb_im)], axis=1).astype(BF16)
    cmat = jnp.concatenate([bd_out(c_re), -bd_out(c_im)], axis=0).astype(BF16)
    a_step = jnp.stack([ab_re.reshape(-1), ab_im.reshape(-1)])
    pr, pi = ab_re, ab_im
    for _ in range(int(math.log2(SSM_SEG))):
        pr, pi = pr * pr - pi * pi, 2.0 * pr * pi
    a_seg = jnp.stack([pr.reshape(-1), pi.reshape(-1)])
    return bmat, cmat, a_step, a_seg


def _ffn_kernel(attn_ref, sgu_ref, ssm_ref, h_ref, p_ref, gmix_ref, wout_ref, gffn_ref, wup_ref, cw_ref, cb_ref,
                wdown_ref, gple_ref, wgate_ref, wproj_ref, gfin_ref, out_ref, xn_scr, tail_scr, shift_scr, act_scr,
                *, final):
    n_chunks = D_FF // FF_CHUNK
    first_tile = pl.program_id(1) == 0

    h = h_ref[0]
    lo = 0
    for part in (attn_ref[0].astype(F32), sgu_ref[0], ssm_ref[0]):
        hi = lo + part.shape[-1]
        h = h + _dot((_rms_scale(part) * gmix_ref[:, lo:hi]).astype(BF16), wout_ref[lo:hi, :])
        lo = hi

    xn_scr[...] = (_rms_scale(h) * gffn_ref[...]).astype(BF16)
    out_ref[0] = h

    def conv(t, col0, slot, buf):
        cols = slice(col0, col0 + FF_CHUNK)
        buf[0:SUBLANES, :] = jnp.where(first_tile, 0.0, tail_scr[slot])
        buf[SUBLANES:, :] = t
        tail_scr[slot] = t[ROW_TILE - SUBLANES:]
        m1 = buf[SUBLANES - 1:SUBLANES - 1 + ROW_TILE, :]
        m2 = buf[SUBLANES - 2:SUBLANES - 2 + ROW_TILE, :]
        return cw_ref[2:3, cols] * t + cw_ref[1:2, cols] * m1 + cw_ref[0:1, cols] * m2 + cb_ref[:, cols]

    for g in range(0, n_chunks, FF_GROUP):
        group = range(g, min(g + FF_GROUP, n_chunks))
        for c in group:
            v0, g0 = c * FF_CHUNK, D_FF + c * FF_CHUNK
            xn = xn_scr[...]
            val = conv(_dot(xn, wup_ref[:, v0:v0 + FF_CHUNK]), v0, 2 * c, shift_scr.at[0])
            gate = conv(_dot(xn, wup_ref[:, g0:g0 + FF_CHUNK]), g0, 2 * c + 1, shift_scr.at[1])
            tanh = jnp.tanh(gate * (GELU_C0 + GELU_C1 * (gate * gate)))
            act_scr[:, (c - g) * FF_CHUNK:(c - g + 1) * FF_CHUNK] = ((gate * val) * (0.5 + 0.5 * tanh)).astype(BF16)
        width = len(group) * FF_CHUNK
        out_ref[0] += _dot(act_scr[:, :width], wdown_ref[g * FF_CHUNK:g * FF_CHUNK + width, :])

    h2 = out_ref[0]
    gate = _sigmoid(_dot((_rms_scale(h2) * gple_ref[...]).astype(BF16), wgate_ref[...]))
    h3 = h2 + gate * _dot(p_ref[...].astype(BF16), wproj_ref[...])
    if final:
        h3 = _rms_scale(h3) * gfin_ref[...]
    out_ref[0] = h3


def _ffn(y_attn, y_sgu, y_ssm, h, p, layer, g_mix, w_out, g_ffn, w_up, conv_w, conv_b, w_down, g_ple, w_gate,
         w_proj, g_final, final):
    B, S, _ = h.shape
    tile = lambda width: pl.BlockSpec((1, ROW_TILE, width), lambda b, s: (b, s, 0))
    return pl.pallas_call(
        functools.partial(_ffn_kernel, final=final),
        grid=(B, S // ROW_TILE),
        in_specs=[tile(ATTN_WIDTH), tile(SGU_WIDTH), tile(SSM_WIDTH), tile(D_MODEL),
                  pl.BlockSpec((None, None, ROW_TILE, PLE_DIM), lambda b, s: (layer, b, s, 0)),
                  _resident((1, D_MODEL)), _resident((D_MODEL, D_MODEL)),
                  _resident((1, D_MODEL)), _resident((D_MODEL, 2 * D_FF)),
                  _resident((3, 2 * D_FF)), _resident((1, 2 * D_FF)), _resident((D_FF, D_MODEL)),
                  _resident((1, D_MODEL)), _resident((D_MODEL, D_MODEL)), _resident((PLE_DIM, D_MODEL)),
                  _resident((1, D_MODEL))],
        out_specs=tile(D_MODEL),
        out_shape=jax.ShapeDtypeStruct(h.shape, F32),
        scratch_shapes=[pltpu.VMEM((ROW_TILE, D_MODEL), BF16),
                        pltpu.VMEM((2 * (D_FF // FF_CHUNK), SUBLANES, FF_CHUNK), F32),
                        pltpu.VMEM((2, SUBLANES + ROW_TILE, FF_CHUNK), F32),
                        pltpu.VMEM((ROW_TILE, FF_GROUP * FF_CHUNK), BF16)],
        compiler_params=_params("parallel", "arbitrary"),
        name="ffn",
    )(y_attn, y_sgu, y_ssm, h, p, g_mix, w_out, g_ffn, w_up, conv_w, conv_b, w_down, g_ple, w_gate, w_proj, g_final)


def kernel(x, p, rel_bias, norm_attn_g, w_in, sgu_ln_g, sgu_ln_b, sgu_w, sgu_b, ssm_a_re, ssm_a_im, ssm_log_dt, ssm_b_re, ssm_b_im, ssm_c_re, ssm_c_im, ssm_d, ssm_glu_w, ssm_glu_b, branch_norm_g, w_out, norm_ffn_g, ffn_w_up, ffn_conv_w, ffn_conv_b, ffn_w_down, norm_ple_g, ple_w_gate, ple_w_proj, final_norm_g):
    row = lambda t: t.reshape(1, -1).astype(F32)
    bias = jnp.stack([_bias_table(rel_bias, window, dil) for window, dil in BRANCHES])
    group_avg = jnp.asarray(np.kron(np.eye(SGU_GROUPS), np.full((SGU_GROUP_WIDTH,) * 2, 1.0 / SGU_GROUP_WIDTH)), BF16)
    causal = np.tril(np.ones((SGU_CHUNK, SGU_CHUNK), dtype=bool))

    h = x
    for i in range(DEPTH):
        w_causal = jnp.where(causal, sgu_w[i].astype(F32), 0.0).astype(BF16)
        b_table = jnp.repeat(sgu_b[i].astype(F32).T, SGU_GROUP_WIDTH, axis=1)
        bmat, cmat, a_step, a_seg = _ssm_matrices(ssm_a_re[i], ssm_a_im[i], ssm_log_dt[i], ssm_b_re[i],
                                                  ssm_b_im[i], ssm_c_re[i], ssm_c_im[i])
        qkv, y_sgu, y_ssm = _mixin(
            h, row(norm_attn_g[i]), w_in[i].astype(BF16),
            (group_avg, row(sgu_ln_g[i]), row(sgu_ln_b[i]), w_causal, b_table),
            (bmat, cmat, a_step, a_seg, row(ssm_d[i]), ssm_glu_w[i].astype(BF16), row(ssm_glu_b[i])))

        y_attn = _attention(qkv, bias)

        h = _ffn(y_attn, y_sgu, y_ssm, h, p, i, row(branch_norm_g[i]), w_out[i].astype(BF16),
                 row(norm_ffn_g[i]), ffn_w_up[i].astype(BF16), ffn_conv_w[i].astype(F32),
                 row(ffn_conv_b[i]), ffn_w_down[i].astype(BF16), row(norm_ple_g[i]),
                 ple_w_gate[i].astype(BF16), ple_w_proj[i].astype(BF16), row(final_norm_g),
                 final=(i == DEPTH - 1))
    return h
```

```python
import functools
import math

import numpy as np
import jax
import jax.numpy as jnp
from jax import lax
from jax.experimental import pallas as pl
from jax.experimental.pallas import tpu as pltpu

D_MODEL = 1024
DEPTH = 2
PLE_DIM = 256
HEAD_DIM = 64
N_HEADS = 8
ATTN_WIDTH = N_HEADS * HEAD_DIM
BRANCHES = ((128, 1), (512, 4), (2048, 16))
ATTN_BLOCK = 128
N_REL_BUCKETS = 32
REL_MAX_DISTANCE = 2048
SGU_GROUPS = 4
SGU_GROUP_WIDTH = 64
SGU_WIDTH = SGU_GROUPS * SGU_GROUP_WIDTH
SGU_CHUNK = 128
SSM_GROUP_CH = 16
SSM_WIDTH = 256
SSM_GROUPS = SSM_WIDTH // SSM_GROUP_CH
SSM_STATE = 64
SSM_LANES = SSM_GROUPS * SSM_STATE
QKV_WIDTH = 3 * ATTN_WIDTH
IN_WIDTH = QKV_WIDTH + 2 * SGU_WIDTH + SSM_WIDTH
D_FF = 2816
EPS = 1e-6
NEG_INF = -1e30
LOG2E = 1.4426950408889634

LANES = 128
SUBLANES = 8
VMEM_LIMIT = 56 * 1024 * 1024

ROW_TILE = 512
ATTN_SUPER = ATTN_BLOCK * max(d for _, d in BRANCHES)
ATTN_STAGE_DIL = 4
SSM_SEG = 64
SSM_TILE = SUBLANES * SSM_SEG
SSM_PITCH = SSM_SEG + SUBLANES
FF_CHUNK = 256
FF_GROUP = 6
GELU_C0 = 0.7978845608028654
GELU_C1 = GELU_C0 * 0.044715

BF16 = jnp.bfloat16
F32 = jnp.float32


def _gelu(x):
    return 0.5 * x * (1.0 + jnp.tanh(0.7978845608028654 * (x + 0.044715 * (x * x * x))))


def _sigmoid(x):
    return 1.0 / (1.0 + jnp.exp(-x))


def _rms_scale(x):
    return x * lax.rsqrt(jnp.mean(x * x, axis=-1, keepdims=True) + EPS)


def _dot(a, b):
    return jnp.dot(a, b, preferred_element_type=F32)


def _params(*sem):
    return pltpu.CompilerParams(dimension_semantics=sem, vmem_limit_bytes=VMEM_LIMIT)


def _resident(shape):
    nd = len(shape)
    return pl.BlockSpec(shape, lambda *_: (0,) * nd, pipeline_mode=pl.Buffered(1))


def _rows(start, size, stride):
    return pl.ds(start, size) if stride == 1 else pl.ds(start, size, stride=stride)


def _mixin_kernel(x_ref, g_ref, w_ref, avg_ref, lng_ref, lnb_ref, wsgu_ref, bt_ref, bmat_ref, cmat_ref, a_ref,
                  aseg_ref, d_ref, gw_ref, gb_ref, qkv_ref, ysgu_ref, yssm_ref, xn_scr, *ssm_scr):
    xn_scr[...] = (_rms_scale(x_ref[0]) * g_ref[...]).astype(BF16)
    o_s = QKV_WIDTH + 2 * SGU_WIDTH
    _ssm_tile(_dot(xn_scr[...], w_ref[:, o_s:]), pl.program_id(1) == 0, bmat_ref, cmat_ref, a_ref, aseg_ref,
              d_ref, gw_ref, gb_ref, yssm_ref, *ssm_scr)
    z_sgu = jnp.concatenate([_dot(xn_scr[...], w_ref[:, QKV_WIDTH + c:QKV_WIDTH + c + 256])
                             for c in range(0, 2 * SGU_WIDTH, 256)], axis=1)
    _sgu_tile(z_sgu, avg_ref, lng_ref, lnb_ref, wsgu_ref, bt_ref, ysgu_ref)
    for c in range(0, QKV_WIDTH, 512):
        qkv_ref[0, :, c:c + 512] = _dot(xn_scr[...], w_ref[:, c:c + 512])


def _mixin(h, g, w, sgu_args, ssm_args):
    B, S, _ = h.shape
    assert SSM_TILE == ROW_TILE and ROW_TILE % SGU_CHUNK == 0
    tile = lambda width: pl.BlockSpec((1, ROW_TILE, width), lambda b, s: (b, s, 0))
    return pl.pallas_call(
        _mixin_kernel,
        grid=(B, S // ROW_TILE),
        in_specs=[tile(D_MODEL), _resident((1, D_MODEL)), _resident((D_MODEL, IN_WIDTH))]
                 + [_resident(t.shape) for t in sgu_args] + [_resident(t.shape) for t in ssm_args],
        out_specs=[tile(QKV_WIDTH), tile(SGU_WIDTH), tile(SSM_WIDTH)],
        out_shape=[jax.ShapeDtypeStruct((B, S, QKV_WIDTH), F32),
                   jax.ShapeDtypeStruct((B, S, SGU_WIDTH), F32),
                   jax.ShapeDtypeStruct((B, S, SSM_WIDTH), F32)],
        scratch_shapes=[pltpu.VMEM((ROW_TILE, D_MODEL), BF16),
                        pltpu.VMEM((SSM_WIDTH // LANES, SUBLANES * SSM_PITCH, LANES), F32),
                        pltpu.VMEM((SSM_TILE, 2 * SSM_LANES), F32), pltpu.VMEM((2, SSM_LANES), F32),
                        pltpu.VMEM((SUBLANES, 2 * SSM_LANES), F32),
                        pltpu.VMEM((SSM_TILE, 2 * SSM_LANES), BF16)],
        compiler_params=_params("parallel", "arbitrary"),
        name="mixin",
    )(h, g, w, *sgu_args, *ssm_args)


def _t5_bucket(dist):
    max_exact = N_REL_BUCKETS // 2
    d = np.maximum(dist, 0)
    large = max_exact + (np.log(np.maximum(d, 1) / max_exact)
                         / np.log(REL_MAX_DISTANCE / max_exact)
                         * (N_REL_BUCKETS - max_exact)).astype(np.int32)
    large = np.minimum(large, N_REL_BUCKETS - 1)
    return np.where(d < max_exact, d, large).astype(np.int32)


def _bias_table(rel_bias, window, dil):
    blk = ATTN_BLOCK
    rel = np.arange(blk)[:, None] + blk - np.arange(2 * blk)[None, :]
    band = (rel >= 0) & (rel <= window // dil)
    bucket = np.where(band, _t5_bucket(rel * dil), -1)[None]
    table = jnp.full((N_HEADS, blk, 2 * blk), NEG_INF, F32)
    for n in range(N_REL_BUCKETS):
        if (bucket == n).any():
            table = jnp.where(bucket == n, (rel_bias[n].astype(F32) * LOG2E)[:, None, None], table)
    no_prev = jnp.where(np.arange(2 * blk)[None, None, :] < blk, NEG_INF, table)
    return jnp.stack([table, no_prev])


def _attn_kernel(q_ref, k_ref, v_ref, bias_ref, o_ref, *scr):
    nb = len(BRANCHES)
    k_scr, v_scr, o_scr, l_scr = scr[0:nb], scr[nb:2 * nb], scr[2 * nb:3 * nb], scr[3 * nb:4 * nb]
    blk = ATTN_BLOCK
    first_super = pl.program_id(2) == 0
    hp = pl.program_id(1)

    stage = scr[4 * nb:4 * nb + 2]
    sd = ATTN_STAGE_DIL
    for bi, (_, d) in enumerate(BRANCHES):
        n_d = ATTN_SUPER // d
        for ti, (src, dst) in enumerate(((k_ref, k_scr[bi]), (v_ref, v_scr[bi]))):
            @pl.when(first_super)
            def _(dst=dst):
                dst[:, 0:blk, :] = jnp.zeros((d, blk, LANES), BF16)

            @pl.when(jnp.logical_not(first_super))
            def _(dst=dst):
                dst[:, 0:blk, :] = dst[:, n_d:n_d + blk, :]

            for r in range(d):
                if d > sd and d % sd == 0:
                    x = stage[ti][r % sd, _rows(r // sd, n_d, d // sd), :]
                else:
                    x = src[_rows(r, n_d, d), :]
                    if d == sd:
                        stage[ti][r] = x
                dst[r, blk:, :] = x.astype(BF16)

    lane = lax.broadcasted_iota(jnp.int32, (blk, LANES), 1)
    low_half = lane < HEAD_DIM
    lane2 = lax.broadcasted_iota(jnp.int32, (2 * blk, LANES), 1)
    low2 = lane2 < HEAD_DIM
    zero_kv = jnp.zeros((2 * blk, LANES), BF16)
    ones_low = jnp.where(low2, 1.0, 0.0).astype(BF16)
    ones_high = jnp.where(low2, 0.0, 1.0).astype(BF16)

    for bi, (_, d) in enumerate(BRANCHES):
        q_blocks = ATTN_SUPER // d // blk

        def unit(u, carry, bi=bi, d=d, q_blocks=q_blocks):
            r = u // q_blocks
            qb = u % q_blocks
            tok = _rows(r + qb * (blk * d), blk, d)
            variant = jnp.logical_and(first_super, qb == 0).astype(jnp.int32)
            q = q_ref[tok, :] * (HEAD_DIM ** -0.5 * LOG2E)
            q2 = jnp.concatenate([jnp.where(low_half, q, 0.0), jnp.where(low_half, 0.0, q)], axis=0).astype(BF16)
            win = pl.ds(pl.multiple_of(qb * blk, blk), 2 * blk)
            kk = k_scr[bi][r, win, :]
            vv = v_scr[bi][r, win, :]
            s = lax.dot_general(q2, kk, (((1,), (1,)), ((), ())), preferred_element_type=F32)
            s = s + jnp.concatenate([bias_ref[bi, variant, 2 * hp], bias_ref[bi, variant, 2 * hp + 1]], axis=0)
            m = jnp.max(s, axis=-1, keepdims=True)
            e = jnp.exp2(s - m).astype(BF16)
            e2 = jnp.concatenate([e[:blk], e[blk:]], axis=1)
            v2 = jnp.concatenate(
                [jnp.concatenate([jnp.where(low2, vv, zero_kv), ones_low], axis=1),
                 jnp.concatenate([jnp.where(low2, zero_kv, vv), ones_high], axis=1)], axis=0)
            oe = _dot(e2, v2)
            den = oe[:, LANES:]
            o_scr[bi][tok, :] = oe[:, :LANES] / den
            l_scr[bi][tok, :] = jnp.where(low_half, m[:blk], m[blk:]) + jnp.log2(den)
            return carry

        lax.fori_loop(0, d * q_blocks, unit, 0, unroll=16)

    def mix(c, carry):
        rows = pl.ds(pl.multiple_of(c * 256, 256), 256)
        ls = [l[rows, :] for l in l_scr]
        m = functools.reduce(jnp.maximum, ls)
        es = [jnp.exp2(l - m) for l in ls]
        num = functools.reduce(lambda a, b: a + b, [e * o[rows, :] for e, o in zip(es, o_scr)])
        o_ref[rows, :] = (num / functools.reduce(lambda a, b: a + b, es)).astype(BF16)
        return carry

    lax.fori_loop(0, ATTN_SUPER // 256, mix, 0)


def _attention(qkv, bias):
    B, S, _ = qkv.shape
    pairs = ATTN_WIDTH // LANES
    slab = lambda part: pl.BlockSpec((None, ATTN_SUPER, LANES), lambda b, hp, s: (b, s, part * pairs + hp))
    kv_scr = [pltpu.VMEM((d, ATTN_BLOCK + ATTN_SUPER // d, LANES), BF16) for _, d in BRANCHES]
    tok_scr = [pltpu.VMEM((ATTN_SUPER, LANES), F32) for _ in BRANCHES]
    stage_scr = pltpu.VMEM((ATTN_STAGE_DIL, ATTN_SUPER // ATTN_STAGE_DIL, LANES), F32)
    return pl.pallas_call(
        _attn_kernel,
        grid=(B, pairs, S // ATTN_SUPER),
        in_specs=[slab(0), slab(1), slab(2), _resident(bias.shape)],
        out_specs=pl.BlockSpec((None, ATTN_SUPER, LANES), lambda b, hp, s: (b, s, hp)),
        out_shape=jax.ShapeDtypeStruct((B, S, ATTN_WIDTH), BF16),
        scratch_shapes=kv_scr + kv_scr + tok_scr + tok_scr + [stage_scr, stage_scr],
        compiler_params=_params("parallel", "parallel", "arbitrary"),
        name="attn",
    )(qkv, qkv, qkv, bias)


def _sgu_tile(z, avg_ref, lng_ref, lnb_ref, w_ref, bt_ref, y_ref):
    z = _gelu(z)
    u = z[:, :SGU_WIDTH]
    v = z[:, SGU_WIDTH:]
    avg = avg_ref[...]

    def group_mean(t):
        hi = t.astype(BF16)
        lo = (t - hi.astype(F32)).astype(BF16)
        return _dot(hi, avg) + _dot(lo, avg)

    d = v - group_mean(v)
    vn = d * lax.rsqrt(group_mean(d * d) + EPS) * lng_ref[...] + lnb_ref[...]
    lane = lax.broadcasted_iota(jnp.int32, (SGU_CHUNK, LANES), 1)
    low_half = lane < SGU_GROUP_WIDTH
    for c in range(ROW_TILE // SGU_CHUNK):
        rows = slice(c * SGU_CHUNK, (c + 1) * SGU_CHUNK)
        for p in range(SGU_GROUPS // 2):
            cols = slice(p * LANES, (p + 1) * LANES)
            vp = vn[rows, cols].astype(BF16)
            mixed = jnp.where(low_half, _dot(w_ref[2 * p], vp), _dot(w_ref[2 * p + 1], vp))
            y_ref[0, rows, cols] = u[rows, cols] * (mixed + bt_ref[:, cols])


def _ssm_tile(z, first_tile, bmat_ref, cmat_ref, a_ref, aseg_ref, d_ref, gw_ref, gb_ref, y_ref,
              perm_scr, x_scr, carry_scr, xin_scr, xb_scr):
    N = SSM_LANES
    slabs = SSM_WIDTH // LANES

    @pl.when(first_tile)
    def _():
        carry_scr[...] = jnp.zeros_like(carry_scr)

    for j in range(SUBLANES):
        for c in range(slabs):
            perm_scr[c, j * SSM_PITCH:j * SSM_PITCH + SSM_SEG, :] = \
                z[j * SSM_SEG:(j + 1) * SSM_SEG, c * LANES:(c + 1) * LANES]
    u = jnp.concatenate(
        [jnp.concatenate([perm_scr[c, pl.ds(t, SUBLANES, stride=SSM_PITCH), :] for c in range(slabs)], axis=1)
         for t in range(SSM_SEG)], axis=0)
    x_scr[...] = _dot(u.astype(BF16), bmat_ref[...])

    a_re = jnp.broadcast_to(a_ref[0:1, :], (SUBLANES, N))
    a_im = jnp.broadcast_to(a_ref[1:2, :], (SUBLANES, N))

    end_re = end_im = jnp.zeros((SUBLANES, N), F32)
    for t in range(SSM_SEG):
        rows = slice(t * SUBLANES, (t + 1) * SUBLANES)
        end_re, end_im = (a_re * end_re - a_im * end_im + x_scr[rows, 0:N],
                          a_re * end_im + a_im * end_re + x_scr[rows, N:2 * N])
        x_scr[rows, 0:N] = end_re
        x_scr[rows, N:2 * N] = end_im

    s_re, s_im = aseg_ref[0:1, :], aseg_ref[1:2, :]
    cr, ci = carry_scr[0:1, :], carry_scr[1:2, :]
    for j in range(SUBLANES):
        xin_scr[j:j + 1, 0:N] = cr
        xin_scr[j:j + 1, N:2 * N] = ci
        er, ei = end_re[j:j + 1, :], end_im[j:j + 1, :]
        cr, ci = s_re * cr - s_im * ci + er, s_re * ci + s_im * cr + ei
    carry_scr[0:1, :] = cr
    carry_scr[1:2, :] = ci

    fr, fi = xin_scr[:, 0:N], xin_scr[:, N:2 * N]
    for i in range(SSM_SEG // 2):
        out_re, out_im = [], []
        for k in range(2):
            fr, fi = a_re * fr - a_im * fi, a_re * fi + a_im * fr
            rows = slice((2 * i + k) * SUBLANES, (2 * i + k + 1) * SUBLANES)
            out_re.append(x_scr[rows, 0:N] + fr)
            out_im.append(x_scr[rows, N:2 * N] + fi)
        rows2 = slice(2 * i * SUBLANES, (2 * i + 2) * SUBLANES)
        xb_scr[rows2, 0:N] = jnp.concatenate(out_re, axis=0).astype(BF16)
        xb_scr[rows2, N:2 * N] = jnp.concatenate(out_im, axis=0).astype(BF16)

    y = _dot(xb_scr[...], cmat_ref[...]) + d_ref[...] * u
    y = _gelu(y)
    y = y * _sigmoid(_dot(y.astype(BF16), gw_ref[...]) + gb_ref[...])
    for t in range(SSM_SEG):
        for c in range(slabs):
            perm_scr[c, pl.ds(t, SUBLANES, stride=SSM_PITCH), :] = \
                y[t * SUBLANES:(t + 1) * SUBLANES, c * LANES:(c + 1) * LANES]
    for j in range(SUBLANES):
        for c in range(slabs):
            y_ref[0, j * SSM_SEG:(j + 1) * SSM_SEG, c * LANES:(c + 1) * LANES] = \
                perm_scr[c, j * SSM_PITCH:j * SSM_PITCH + SSM_SEG, :]


def _ssm_matrices(a_re, a_im, log_dt, b_re, b_im, c_re, c_im):
    G, N, C = SSM_GROUPS, SSM_STATE, SSM_GROUP_CH
    dt = jnp.exp(log_dt)[:, None]
    mag = jnp.exp(a_re * dt)
    ab_re, ab_im = mag * jnp.cos(a_im * dt), mag * jnp.sin(a_im * dt)
    den = a_re * a_re + a_im * a_im
    f_re = ((ab_re - 1.0) * a_re + ab_im * a_im) / den
    f_im = (ab_im * a_re - (ab_re - 1.0) * a_im) / den
    bb_re = f_re[:, :, None] * b_re - f_im[:, :, None] * b_im
    bb_im = f_re[:, :, None] * b_im + f_im[:, :, None] * b_re
    eye = jnp.eye(G, dtype=F32)
    bd_in = lambda t: jnp.einsum('gnc,gh->gchn', t, eye).reshape(G * C, G * N)
    bd_out = lambda t: jnp.einsum('gcn,gh->gnhc', t, eye).reshape(G * N, G * C)
    bmat = jnp.concatenate([bd_in(bb_re), bd_in(bb_im)], axis=1).astype(BF16)
    cmat = jnp.concatenate([bd_out(c_re), -bd_out(c_im)], axis=0).astype(BF16)
    a_step = jnp.stack([ab_re.reshape(-1), ab_im.reshape(-1)])
    pr, pi = ab_re, ab_im
    for _ in range(int(math.log2(SSM_SEG))):
        pr, pi = pr * pr - pi * pi, 2.0 * pr * pi
    a_seg = jnp.stack([pr.reshape(-1), pi.reshape(-1)])
    return bmat, cmat, a_step, a_seg


def _ffn_kernel(attn_ref, sgu_ref, ssm_ref, h_ref, p_ref, gmix_ref, wout_ref, gffn_ref, wup_ref, cw_ref, cb_ref,
                wdown_ref, gple_ref, wgate_ref, wproj_ref, gfin_ref, out_ref, xn_scr, tail_scr, shift_scr, act_scr,
                *, final):
    n_chunks = D_FF // FF_CHUNK
    first_tile = pl.program_id(1) == 0

    h = h_ref[0]
    lo = 0
    for part in (attn_ref[0].astype(F32), sgu_ref[0], ssm_ref[0]):
        hi = lo + part.shape[-1]
        h = h + _dot((_rms_scale(part) * gmix_ref[:, lo:hi]).astype(BF16), wout_ref[lo:hi, :])
        lo = hi

    xn_scr[...] = (_rms_scale(h) * gffn_ref[...]).astype(BF16)
    out_ref[0] = h

    def conv(t, col0, slot, buf):
        cols = slice(col0, col0 + FF_CHUNK)
        buf[0:SUBLANES, :] = jnp.where(first_tile, 0.0, tail_scr[slot])
        buf[SUBLANES:, :] = t
        tail_scr[slot] = t[ROW_TILE - SUBLANES:]
        m1 = buf[SUBLANES - 1:SUBLANES - 1 + ROW_TILE, :]
        m2 = buf[SUBLANES - 2:SUBLANES - 2 + ROW_TILE, :]
        return cw_ref[2:3, cols] * t + cw_ref[1:2, cols] * m1 + cw_ref[0:1, cols] * m2 + cb_ref[:, cols]

    def down(g, width, act):
        out_ref[0] += _dot(act[:, :width], wdown_ref[g * FF_CHUNK:g * FF_CHUNK + width, :])

    pending = None
    for gi, g in enumerate(range(0, n_chunks, FF_GROUP)):
        group = range(g, min(g + FF_GROUP, n_chunks))
        act = act_scr.at[gi % 2]
        for c in group:
            v0, g0 = c * FF_CHUNK, D_FF + c * FF_CHUNK
            xn = xn_scr[...]
            val = conv(_dot(xn, wup_ref[:, v0:v0 + FF_CHUNK]), v0, 2 * c, shift_scr.at[2 * (c % 2)])
            gate = conv(_dot(xn, wup_ref[:, g0:g0 + FF_CHUNK]), g0, 2 * c + 1, shift_scr.at[2 * (c % 2) + 1])
            tanh = jnp.tanh(gate * (GELU_C0 + GELU_C1 * (gate * gate)))
            act[:, (c - g) * FF_CHUNK:(c - g + 1) * FF_CHUNK] = ((gate * val) * (0.5 + 0.5 * tanh)).astype(BF16)
            if pending is not None and c == g:
                down(*pending)
                pending = None
        pending = (g, len(group) * FF_CHUNK, act)
    down(*pending)

    h2 = out_ref[0]
    gate = _sigmoid(_dot((_rms_scale(h2) * gple_ref[...]).astype(BF16), wgate_ref[...]))
    h3 = h2 + gate * _dot(p_ref[...].astype(BF16), wproj_ref[...])
    if final:
        h3 = _rms_scale(h3) * gfin_ref[...]
    out_ref[0] = h3


def _ffn(y_attn, y_sgu, y_ssm, h, p, layer, g_mix, w_out, g_ffn, w_up, conv_w, conv_b, w_down, g_ple, w_gate,
         w_proj, g_final, final):
    B, S, _ = h.shape
    tile = lambda width: pl.BlockSpec((1, ROW_TILE, width), lambda b, s: (b, s, 0))
    return pl.pallas_call(
        functools.partial(_ffn_kernel, final=final),
        grid=(B, S // ROW_TILE),
        in_specs=[tile(ATTN_WIDTH), tile(SGU_WIDTH), tile(SSM_WIDTH), tile(D_MODEL),
                  pl.BlockSpec((None, None, ROW_TILE, PLE_DIM), lambda b, s: (layer, b, s, 0)),
                  _resident((1, D_MODEL)), _resident((D_MODEL, D_MODEL)),
                  _resident((1, D_MODEL)), _resident((D_MODEL, 2 * D_FF)),
                  _resident((3, 2 * D_FF)), _resident((1, 2 * D_FF)), _resident((D_FF, D_MODEL)),
                  _resident((1, D_MODEL)), _resident((D_MODEL, D_MODEL)), _resident((PLE_DIM, D_MODEL)),
                  _resident((1, D_MODEL))],
        out_specs=tile(D_MODEL),
        out_shape=jax.ShapeDtypeStruct(h.shape, F32),
        scratch_shapes=[pltpu.VMEM((ROW_TILE, D_MODEL), BF16),
                        pltpu.VMEM((2 * (D_FF // FF_CHUNK), SUBLANES, FF_CHUNK), F32),
                        pltpu.VMEM((4, SUBLANES + ROW_TILE, FF_CHUNK), F32),
                        pltpu.VMEM((2, ROW_TILE, FF_GROUP * FF_CHUNK), BF16)],
        compiler_params=_params("parallel", "arbitrary"),
        name="ffn",
    )(y_attn, y_sgu, y_ssm, h, p, g_mix, w_out, g_ffn, w_up, conv_w, conv_b, w_down, g_ple, w_gate, w_proj, g_final)


def kernel(x, p, rel_bias, norm_attn_g, w_in, sgu_ln_g, sgu_ln_b, sgu_w, sgu_b, ssm_a_re, ssm_a_im, ssm_log_dt, ssm_b_re, ssm_b_im, ssm_c_re, ssm_c_im, ssm_d, ssm_glu_w, ssm_glu_b, branch_norm_g, w_out, norm_ffn_g, ffn_w_up, ffn_conv_w, ffn_conv_b, ffn_w_down, norm_ple_g, ple_w_gate, ple_w_proj, final_norm_g):
    row = lambda t: t.reshape(1, -1).astype(F32)
    bias = jnp.stack([_bias_table(rel_bias, window, dil) for window, dil in BRANCHES])
    group_avg = jnp.asarray(np.kron(np.eye(SGU_GROUPS), np.full((SGU_GROUP_WIDTH,) * 2, 1.0 / SGU_GROUP_WIDTH)), BF16)
    causal = np.tril(np.ones((SGU_CHUNK, SGU_CHUNK), dtype=bool))

    h = x
    for i in range(DEPTH):
        w_causal = jnp.where(causal, sgu_w[i].astype(F32), 0.0).astype(BF16)
        b_table = jnp.repeat(sgu_b[i].astype(F32).T, SGU_GROUP_WIDTH, axis=1)
        bmat, cmat, a_step, a_seg = _ssm_matrices(ssm_a_re[i], ssm_a_im[i], ssm_log_dt[i], ssm_b_re[i],
                                                  ssm_b_im[i], ssm_c_re[i], ssm_c_im[i])
        qkv, y_sgu, y_ssm = _mixin(
            h, row(norm_attn_g[i]), w_in[i].astype(BF16),
            (group_avg, row(sgu_ln_g[i]), row(sgu_ln_b[i]), w_causal, b_table),
            (bmat, cmat, a_step, a_seg, row(ssm_d[i]), ssm_glu_w[i].astype(BF16), row(ssm_glu_b[i])))

        y_attn = _attention(qkv, bias)

        h = _ffn(y_attn, y_sgu, y_ssm, h, p, i, row(branch_norm_g[i]), w_out[i].astype(BF16),
                 row(norm_ffn_g[i]), ffn_w_up[i].astype(BF16), ffn_conv_w[i].astype(F32),
                 row(ffn_conv_b[i]), ffn_w_down[i].astype(BF16), row(norm_ple_g[i]),
                 ple_w_gate[i].astype(BF16), ple_w_proj[i].astype(BF16), row(final_norm_g),
                 final=(i == DEPTH - 1))
    return h
```

```python
import functools
import math

import numpy as np
import jax
import jax.numpy as jnp
from jax import lax
from jax.experimental import pallas as pl
from jax.experimental.pallas import tpu as pltpu

D_MODEL = 1024
DEPTH = 2
PLE_DIM = 256
HEAD_DIM = 64
N_HEADS = 8
ATTN_WIDTH = N_HEADS * HEAD_DIM
BRANCHES = ((128, 1), (512, 4), (2048, 16))
ATTN_BLOCK = 128
N_REL_BUCKETS = 32
REL_MAX_DISTANCE = 2048
SGU_GROUPS = 4
SGU_GROUP_WIDTH = 64
SGU_WIDTH = SGU_GROUPS * SGU_GROUP_WIDTH
SGU_CHUNK = 128
SSM_GROUP_CH = 16
SSM_WIDTH = 256
SSM_GROUPS = SSM_WIDTH // SSM_GROUP_CH
SSM_STATE = 64
SSM_LANES = SSM_GROUPS * SSM_STATE
QKV_WIDTH = 3 * ATTN_WIDTH
IN_WIDTH = QKV_WIDTH + 2 * SGU_WIDTH + SSM_WIDTH
D_FF = 2816
EPS = 1e-6
NEG_INF = -1e30
LOG2E = 1.4426950408889634

LANES = 128
SUBLANES = 8
VMEM_LIMIT = 56 * 1024 * 1024

ROW_TILE = 512
ATTN_SUPER = ATTN_BLOCK * max(d for _, d in BRANCHES)
ATTN_STAGE_DIL = 4
SSM_SEG = 64
SSM_TILE = SUBLANES * SSM_SEG
SSM_PITCH = SSM_SEG + SUBLANES
FF_CHUNK = 256
FF_GROUP = 6
GELU_C0 = 0.7978845608028654
GELU_C1 = GELU_C0 * 0.044715

BF16 = jnp.bfloat16
F32 = jnp.float32


def _gelu(x):
    return 0.5 * x * (1.0 + jnp.tanh(0.7978845608028654 * (x + 0.044715 * (x * x * x))))


def _sigmoid(x):
    return 1.0 / (1.0 + jnp.exp(-x))


def _rms_scale(x):
    return x * lax.rsqrt(jnp.mean(x * x, axis=-1, keepdims=True) + EPS)


def _dot(a, b):
    return jnp.dot(a, b, preferred_element_type=F32)


def _params(*sem):
    return pltpu.CompilerParams(dimension_semantics=sem, vmem_limit_bytes=VMEM_LIMIT)


def _resident(shape):
    nd = len(shape)
    return pl.BlockSpec(shape, lambda *_: (0,) * nd, pipeline_mode=pl.Buffered(1))


def _rows(start, size, stride):
    return pl.ds(start, size) if stride == 1 else pl.ds(start, size, stride=stride)


def _mixin_kernel(x_ref, g_ref, w_ref, avg_ref, lng_ref, lnb_ref, wsgu_ref, bt_ref, bmat_ref, cmat_ref, a_ref,
                  aseg_ref, d_ref, gw_ref, gb_ref, qkv_ref, ysgu_ref, yssm_ref, xn_scr, *ssm_scr):
    xn_scr[...] = (_rms_scale(x_ref[0]) * g_ref[...]).astype(BF16)
    o_s = QKV_WIDTH + 2 * SGU_WIDTH

    def qkv_part(c):
        def emit():
            qkv_ref[0, :, c:c + 512] = _dot(xn_scr[...], w_ref[:, c:c + 512])
        return emit

    def sgu_part():
        z_sgu = jnp.concatenate([_dot(xn_scr[...], w_ref[:, QKV_WIDTH + c:QKV_WIDTH + c + 256])
                                 for c in range(0, 2 * SGU_WIDTH, 256)], axis=1)
        _sgu_tile(z_sgu, avg_ref, lng_ref, lnb_ref, wsgu_ref, bt_ref, ysgu_ref)

    _ssm_tile(_dot(xn_scr[...], w_ref[:, o_s:]), pl.program_id(1) == 0, bmat_ref, cmat_ref, a_ref, aseg_ref,
              d_ref, gw_ref, gb_ref, yssm_ref, *ssm_scr,
              fillers=[qkv_part(c) for c in range(0, QKV_WIDTH, 512)] + [sgu_part])


def _mixin(h, g, w, sgu_args, ssm_args):
    B, S, _ = h.shape
    assert SSM_TILE == ROW_TILE and ROW_TILE % SGU_CHUNK == 0
    tile = lambda width: pl.BlockSpec((1, ROW_TILE, width), lambda b, s: (b, s, 0))
    return pl.pallas_call(
        _mixin_kernel,
        grid=(B, S // ROW_TILE),
        in_specs=[tile(D_MODEL), _resident((1, D_MODEL)), _resident((D_MODEL, IN_WIDTH))]
                 + [_resident(t.shape) for t in sgu_args] + [_resident(t.shape) for t in ssm_args],
        out_specs=[tile(QKV_WIDTH), tile(SGU_WIDTH), tile(SSM_WIDTH)],
        out_shape=[jax.ShapeDtypeStruct((B, S, QKV_WIDTH), F32),
                   jax.ShapeDtypeStruct((B, S, SGU_WIDTH), F32),
                   jax.ShapeDtypeStruct((B, S, SSM_WIDTH), F32)],
        scratch_shapes=[pltpu.VMEM((ROW_TILE, D_MODEL), BF16),
                        pltpu.VMEM((SSM_WIDTH // LANES, SUBLANES * SSM_PITCH, LANES), F32),
                        pltpu.VMEM((SSM_TILE, 2 * SSM_LANES), F32), pltpu.VMEM((2, SSM_LANES), F32),
                        pltpu.VMEM((SUBLANES, 2 * SSM_LANES), F32),
                        pltpu.VMEM((SSM_TILE, 2 * SSM_LANES), BF16)],
        compiler_params=_params("parallel", "arbitrary"),
        name="mixin",
    )(h, g, w, *sgu_args, *ssm_args)


def _t5_bucket(dist):
    max_exact = N_REL_BUCKETS // 2
    d = np.maximum(dist, 0)
    large = max_exact + (np.log(np.maximum(d, 1) / max_exact)
                         / np.log(REL_MAX_DISTANCE / max_exact)
                         * (N_REL_BUCKETS - max_exact)).astype(np.int32)
    large = np.minimum(large, N_REL_BUCKETS - 1)
    return np.where(d < max_exact, d, large).astype(np.int32)


def _bias_table(rel_bias, window, dil):
    blk = ATTN_BLOCK
    rel = np.arange(blk)[:, None] + blk - np.arange(2 * blk)[None, :]
    band = (rel >= 0) & (rel <= window // dil)
    bucket = np.where(band, _t5_bucket(rel * dil), -1)[None]
    table = jnp.full((N_HEADS, blk, 2 * blk), NEG_INF, F32)
    for n in range(N_REL_BUCKETS):
        if (bucket == n).any():
            table = jnp.where(bucket == n, (rel_bias[n].astype(F32) * LOG2E)[:, None, None], table)
    no_prev = jnp.where(np.arange(2 * blk)[None, None, :] < blk, NEG_INF, table)
    return jnp.stack([table, no_prev])


def _attn_kernel(q_ref, k_ref, v_ref, bias_ref, o_ref, *scr):
    nb = len(BRANCHES)
    k_scr, v_scr, o_scr, l_scr = scr[0:nb], scr[nb:2 * nb], scr[2 * nb:3 * nb], scr[3 * nb:4 * nb]
    blk = ATTN_BLOCK
    first_super = pl.program_id(2) == 0
    hp = pl.program_id(1)

    stage = scr[4 * nb:4 * nb + 2]
    sd = ATTN_STAGE_DIL
    for bi, (_, d) in enumerate(BRANCHES):
        n_d = ATTN_SUPER // d
        for ti, (src, dst) in enumerate(((k_ref, k_scr[bi]), (v_ref, v_scr[bi]))):
            @pl.when(first_super)
            def _(dst=dst):
                dst[:, 0:blk, :] = jnp.zeros((d, blk, LANES), BF16)

            @pl.when(jnp.logical_not(first_super))
            def _(dst=dst):
                dst[:, 0:blk, :] = dst[:, n_d:n_d + blk, :]

            for r in range(d):
                if d > sd and d % sd == 0:
                    x = stage[ti][r % sd, _rows(r // sd, n_d, d // sd), :]
                else:
                    x = src[_rows(r, n_d, d), :]
                    if d == sd:
                        stage[ti][r] = x
                dst[r, blk:, :] = x.astype(BF16)

    lane = lax.broadcasted_iota(jnp.int32, (blk, LANES), 1)
    low_half = lane < HEAD_DIM
    lane2 = lax.broadcasted_iota(jnp.int32, (2 * blk, LANES), 1)
    low2 = lane2 < HEAD_DIM
    zero_kv = jnp.zeros((2 * blk, LANES), BF16)
    ones_low = jnp.where(low2, 1.0, 0.0).astype(BF16)
    ones_high = jnp.where(low2, 0.0, 1.0).astype(BF16)

    for bi, (_, d) in enumerate(BRANCHES):
        q_blocks = ATTN_SUPER // d // blk

        def unit(u, carry, bi=bi, d=d, q_blocks=q_blocks):
            r = u // q_blocks
            qb = u % q_blocks
            tok = _rows(r + qb * (blk * d), blk, d)
            variant = jnp.logical_and(first_super, qb == 0).astype(jnp.int32)
            q = q_ref[tok, :] * (HEAD_DIM ** -0.5 * LOG2E)
            q2 = jnp.concatenate([jnp.where(low_half, q, 0.0), jnp.where(low_half, 0.0, q)], axis=0).astype(BF16)
            win = pl.ds(pl.multiple_of(qb * blk, blk), 2 * blk)
            kk = k_scr[bi][r, win, :]
            vv = v_scr[bi][r, win, :]
            s = lax.dot_general(q2, kk, (((1,), (1,)), ((), ())), preferred_element_type=F32)
            s = s + jnp.concatenate([bias_ref[bi, variant, 2 * hp], bias_ref[bi, variant, 2 * hp + 1]], axis=0)
            m = jnp.max(s, axis=-1, keepdims=True)
            e = jnp.exp2(s - m).astype(BF16)
            e2 = jnp.concatenate([e[:blk], e[blk:]], axis=1)
            v2 = jnp.concatenate(
                [jnp.concatenate([jnp.where(low2, vv, zero_kv), ones_low], axis=1),
                 jnp.concatenate([jnp.where(low2, zero_kv, vv), ones_high], axis=1)], axis=0)
            oe = _dot(e2, v2)
            den = oe[:, LANES:]
            o_scr[bi][tok, :] = oe[:, :LANES] / den
            l_scr[bi][tok, :] = jnp.where(low_half, m[:blk], m[blk:]) + jnp.log2(den)
            return carry

        lax.fori_loop(0, d * q_blocks, unit, 0, unroll=16)

    def mix(c, carry):
        rows = pl.ds(pl.multiple_of(c * 256, 256), 256)
        ls = [l[rows, :] for l in l_scr]
        m = functools.reduce(jnp.maximum, ls)
        es = [jnp.exp2(l - m) for l in ls]
        num = functools.reduce(lambda a, b: a + b, [e * o[rows, :] for e, o in zip(es, o_scr)])
        o_ref[rows, :] = (num / functools.reduce(lambda a, b: a + b, es)).astype(BF16)
        return carry

    lax.fori_loop(0, ATTN_SUPER // 256, mix, 0)


def _attention(qkv, bias):
    B, S, _ = qkv.shape
    pairs = ATTN_WIDTH // LANES
    slab = lambda part: pl.BlockSpec((None, ATTN_SUPER, LANES), lambda b, hp, s: (b, s, part * pairs + hp))
    kv_scr = [pltpu.VMEM((d, ATTN_BLOCK + ATTN_SUPER // d, LANES), BF16) for _, d in BRANCHES]
    tok_scr = [pltpu.VMEM((ATTN_SUPER, LANES), F32) for _ in BRANCHES]
    stage_scr = pltpu.VMEM((ATTN_STAGE_DIL, ATTN_SUPER // ATTN_STAGE_DIL, LANES), F32)
    return pl.pallas_call(
        _attn_kernel,
        grid=(B, pairs, S // ATTN_SUPER),
        in_specs=[slab(0), slab(1), slab(2), _resident(bias.shape)],
        out_specs=pl.BlockSpec((None, ATTN_SUPER, LANES), lambda b, hp, s: (b, s, hp)),
        out_shape=jax.ShapeDtypeStruct((B, S, ATTN_WIDTH), BF16),
        scratch_shapes=kv_scr + kv_scr + tok_scr + tok_scr + [stage_scr, stage_scr],
        compiler_params=_params("parallel", "parallel", "arbitrary"),
        name="attn",
    )(qkv, qkv, qkv, bias)


def _sgu_tile(z, avg_ref, lng_ref, lnb_ref, w_ref, bt_ref, y_ref):
    z = _gelu(z)
    u = z[:, :SGU_WIDTH]
    v = z[:, SGU_WIDTH:]
    avg = avg_ref[...]

    def group_mean(t):
        hi = t.astype(BF16)
        lo = (t - hi.astype(F32)).astype(BF16)
        return _dot(hi, avg) + _dot(lo, avg)

    d = v - group_mean(v)
    vn = d * lax.rsqrt(group_mean(d * d) + EPS) * lng_ref[...] + lnb_ref[...]
    lane = lax.broadcasted_iota(jnp.int32, (SGU_CHUNK, LANES), 1)
    low_half = lane < SGU_GROUP_WIDTH
    for c in range(ROW_TILE // SGU_CHUNK):
        rows = slice(c * SGU_CHUNK, (c + 1) * SGU_CHUNK)
        for p in range(SGU_GROUPS // 2):
            cols = slice(p * LANES, (p + 1) * LANES)
            vp = vn[rows, cols].astype(BF16)
            mixed = jnp.where(low_half, _dot(w_ref[2 * p], vp), _dot(w_ref[2 * p + 1], vp))
            y_ref[0, rows, cols] = u[rows, cols] * (mixed + bt_ref[:, cols])


def _ssm_tile(z, first_tile, bmat_ref, cmat_ref, a_ref, aseg_ref, d_ref, gw_ref, gb_ref, y_ref,
              perm_scr, x_scr, carry_scr, xin_scr, xb_scr, fillers):
    N = SSM_LANES
    slabs = SSM_WIDTH // LANES

    @pl.when(first_tile)
    def _():
        carry_scr[...] = jnp.zeros_like(carry_scr)

    fillers[0]()
    for j in range(SUBLANES):
        for c in range(slabs):
            perm_scr[c, j * SSM_PITCH:j * SSM_PITCH + SSM_SEG, :] = \
                z[j * SSM_SEG:(j + 1) * SSM_SEG, c * LANES:(c + 1) * LANES]
    u = jnp.concatenate(
        [jnp.concatenate([perm_scr[c, pl.ds(t, SUBLANES, stride=SSM_PITCH), :] for c in range(slabs)], axis=1)
         for t in range(SSM_SEG)], axis=0)
    x_scr[...] = _dot(u.astype(BF16), bmat_ref[...])
    fillers[1]()

    a_re = jnp.broadcast_to(a_ref[0:1, :], (SUBLANES, N))
    a_im = jnp.broadcast_to(a_ref[1:2, :], (SUBLANES, N))

    end_re = end_im = jnp.zeros((SUBLANES, N), F32)
    for t in range(SSM_SEG):
        rows = slice(t * SUBLANES, (t + 1) * SUBLANES)
        end_re, end_im = (a_re * end_re - a_im * end_im + x_scr[rows, 0:N],
                          a_re * end_im + a_im * end_re + x_scr[rows, N:2 * N])
        x_scr[rows, 0:N] = end_re
        x_scr[rows, N:2 * N] = end_im

    s_re, s_im = aseg_ref[0:1, :], aseg_ref[1:2, :]
    cr, ci = carry_scr[0:1, :], carry_scr[1:2, :]
    for j in range(SUBLANES):
        xin_scr[j:j + 1, 0:N] = cr
        xin_scr[j:j + 1, N:2 * N] = ci
        er, ei = end_re[j:j + 1, :], end_im[j:j + 1, :]
        cr, ci = s_re * cr - s_im * ci + er, s_re * ci + s_im * cr + ei
    carry_scr[0:1, :] = cr
    carry_scr[1:2, :] = ci

    fillers[2]()
    fr, fi = xin_scr[:, 0:N], xin_scr[:, N:2 * N]
    for i in range(SSM_SEG // 2):
        out_re, out_im = [], []
        for k in range(2):
            fr, fi = a_re * fr - a_im * fi, a_re * fi + a_im * fr
            rows = slice((2 * i + k) * SUBLANES, (2 * i + k + 1) * SUBLANES)
            out_re.append(x_scr[rows, 0:N] + fr)
            out_im.append(x_scr[rows, N:2 * N] + fi)
        rows2 = slice(2 * i * SUBLANES, (2 * i + 2) * SUBLANES)
        xb_scr[rows2, 0:N] = jnp.concatenate(out_re, axis=0).astype(BF16)
        xb_scr[rows2, N:2 * N] = jnp.concatenate(out_im, axis=0).astype(BF16)

    fillers[3]()
    y = _dot(xb_scr[...], cmat_ref[...]) + d_ref[...] * u
    y = _gelu(y)
    y = y * _sigmoid(_dot(y.astype(BF16), gw_ref[...]) + gb_ref[...])
    for t in range(SSM_SEG):
        for c in range(slabs):
            perm_scr[c, pl.ds(t, SUBLANES, stride=SSM_PITCH), :] = \
                y[t * SUBLANES:(t + 1) * SUBLANES, c * LANES:(c + 1) * LANES]
    for j in range(SUBLANES):
        for c in range(slabs):
            y_ref[0, j * SSM_SEG:(j + 1) * SSM_SEG, c * LANES:(c + 1) * LANES] = \
                perm_scr[c, j * SSM_PITCH:j * SSM_PITCH + SSM_SEG, :]


def _ssm_matrices(a_re, a_im, log_dt, b_re, b_im, c_re, c_im):
    G, N, C = SSM_GROUPS, SSM_STATE, SSM_GROUP_CH
    dt = jnp.exp(log_dt)[:, None]
    mag = jnp.exp(a_re * dt)
    ab_re, ab_im = mag * jnp.cos(a_im * dt), mag * jnp.sin(a_im * dt)
    den = a_re * a_re + a_im * a_im
    f_re = ((ab_re - 1.0) * a_re + ab_im * a_im) / den
    f_im = (ab_im * a_re - (ab_re - 1.0) * a_im) / den
    bb_re = f_re[:, :, None] * b_re - f_im[:, :, None] * b_im
    bb_im = f_re[:, :, None] * b_im + f_im[:, :, None] * b_re
    eye = jnp.eye(G, dtype=F32)
    bd_in = lambda t: jnp.einsum('gnc,gh->gchn', t, eye).reshape(G * C, G * N)
    bd_out = lambda t: jnp.einsum('gcn,gh->gnhc', t, eye).reshape(G * N, G * C)
    bmat = jnp.concatenate([bd_in(bb_re), bd_in(bb_im)], axis=1).astype(BF16)
    cmat = jnp.concatenate([bd_out(c_re), -bd_out(c_im)], axis=0).astype(BF16)
    a_step = jnp.stack([ab_re.reshape(-1), ab_im.reshape(-1)])
    pr, pi = ab_re, ab_im
    for _ in range(int(math.log2(SSM_SEG))):
        pr, pi = pr * pr - pi * pi, 2.0 * pr * pi
    a_seg = jnp.stack([pr.reshape(-1), pi.reshape(-1)])
    return bmat, cmat, a_step, a_seg


def _ffn_kernel(attn_ref, sgu_ref, ssm_ref, h_ref, p_ref, gmix_ref, wout_ref, gffn_ref, wup_ref, cw_ref, cb_ref,
                wdown_ref, gple_ref, wgate_ref, wproj_ref, gfin_ref, out_ref, xn_scr, tail_scr, shift_scr, act_scr,
                *, final):
    n_chunks = D_FF // FF_CHUNK
    first_tile = pl.program_id(1) == 0

    h = h_ref[0]
    lo = 0
    for part in (attn_ref[0].astype(F32), sgu_ref[0], ssm_ref[0]):
        hi = lo + part.shape[-1]
        h = h + _dot((_rms_scale(part) * gmix_ref[:, lo:hi]).astype(BF16), wout_ref[lo:hi, :])
        lo = hi

    xn_scr[...] = (_rms_scale(h) * gffn_ref[...]).astype(BF16)
    out_ref[0] = h

    def conv(t, col0, slot, buf):
        cols = slice(col0, col0 + FF_CHUNK)
        buf[0:SUBLANES, :] = jnp.where(first_tile, 0.0, tail_scr[slot])
        buf[SUBLANES:, :] = t
        tail_scr[slot] = t[ROW_TILE - SUBLANES:]
        m1 = buf[SUBLANES - 1:SUBLANES - 1 + ROW_TILE, :]
        m2 = buf[SUBLANES - 2:SUBLANES - 2 + ROW_TILE, :]
        return cw_ref[2:3, cols] * t + cw_ref[1:2, cols] * m1 + cw_ref[0:1, cols] * m2 + cb_ref[:, cols]

    def down(g, width, act):
        out_ref[0] += _dot(act[:, :width], wdown_ref[g * FF_CHUNK:g * FF_CHUNK + width, :])

    pending = None
    for gi, g in enumerate(range(0, n_chunks, FF_GROUP)):
        group = range(g, min(g + FF_GROUP, n_chunks))
        act = act_scr.at[gi % 2]
        for c in group:
            v0, g0 = c * FF_CHUNK, D_FF + c * FF_CHUNK
            xn = xn_scr[...]
            val = conv(_dot(xn, wup_ref[:, v0:v0 + FF_CHUNK]), v0, 2 * c, shift_scr.at[2 * (c % 2)])
            gate = conv(_dot(xn, wup_ref[:, g0:g0 + FF_CHUNK]), g0, 2 * c + 1, shift_scr.at[2 * (c % 2) + 1])
            tanh = jnp.tanh(gate * (GELU_C0 + GELU_C1 * (gate * gate)))
            act[:, (c - g) * FF_CHUNK:(c - g + 1) * FF_CHUNK] = ((gate * val) * (0.5 + 0.5 * tanh)).astype(BF16)
            if pending is not None and c == g:
                down(*pending)
                pending = None
        pending = (g, len(group) * FF_CHUNK, act)
    down(*pending)

    h2 = out_ref[0]
    gate = _sigmoid(_dot((_rms_scale(h2) * gple_ref[...]).astype(BF16), wgate_ref[...]))
    h3 = h2 + gate * _dot(p_ref[...].astype(BF16), wproj_ref[...])
    if final:
        h3 = _rms_scale(h3) * gfin_ref[...]
    out_ref[0] = h3


def _ffn(y_attn, y_sgu, y_ssm, h, p, layer, g_mix, w_out, g_ffn, w_up, conv_w, conv_b, w_down, g_ple, w_gate,
         w_proj, g_final, final):
    B, S, _ = h.shape
    tile = lambda width: pl.BlockSpec((1, ROW_TILE, width), lambda b, s: (b, s, 0))
    return pl.pallas_call(
        functools.partial(_ffn_kernel, final=final),
        grid=(B, S // ROW_TILE),
        in_specs=[tile(ATTN_WIDTH), tile(SGU_WIDTH), tile(SSM_WIDTH), tile(D_MODEL),
                  pl.BlockSpec((None, None, ROW_TILE, PLE_DIM), lambda b, s: (layer, b, s, 0)),
                  _resident((1, D_MODEL)), _resident((D_MODEL, D_MODEL)),
                  _resident((1, D_MODEL)), _resident((D_MODEL, 2 * D_FF)),
                  _resident((3, 2 * D_FF)), _resident((1, 2 * D_FF)), _resident((D_FF, D_MODEL)),
                  _resident((1, D_MODEL)), _resident((D_MODEL, D_MODEL)), _resident((PLE_DIM, D_MODEL)),
                  _resident((1, D_MODEL))],
        out_specs=tile(D_MODEL),
        out_shape=jax.ShapeDtypeStruct(h.shape, F32),
        scratch_shapes=[pltpu.VMEM((ROW_TILE, D_MODEL), BF16),
                        pltpu.VMEM((2 * (D_FF // FF_CHUNK), SUBLANES, FF_CHUNK), F32),
                        pltpu.VMEM((4, SUBLANES + ROW_TILE, FF_CHUNK), F32),
                        pltpu.VMEM((2, ROW_TILE, FF_GROUP * FF_CHUNK), BF16)],
        compiler_params=_params("parallel", "arbitrary"),
        name="ffn",
    )(y_attn, y_sgu, y_ssm, h, p, g_mix, w_out, g_ffn, w_up, conv_w, conv_b, w_down, g_ple, w_gate, w_proj, g_final)


def kernel(x, p, rel_bias, norm_attn_g, w_in, sgu_ln_g, sgu_ln_b, sgu_w, sgu_b, ssm_a_re, ssm_a_im, ssm_log_dt, ssm_b_re, ssm_b_im, ssm_c_re, ssm_c_im, ssm_d, ssm_glu_w, ssm_glu_b, branch_norm_g, w_out, norm_ffn_g, ffn_w_up, ffn_conv_w, ffn_conv_b, ffn_w_down, norm_ple_g, ple_w_gate, ple_w_proj, final_norm_g):
    row = lambda t: t.reshape(1, -1).astype(F32)
    bias = jnp.stack([_bias_table(rel_bias, window, dil) for window, dil in BRANCHES])
    group_avg = jnp.asarray(np.kron(np.eye(SGU_GROUPS), np.full((SGU_GROUP_WIDTH,) * 2, 1.0 / SGU_GROUP_WIDTH)), BF16)
    causal = np.tril(np.ones((SGU_CHUNK, SGU_CHUNK), dtype=bool))

    h = x
    for i in range(DEPTH):
        w_causal = jnp.where(causal, sgu_w[i].astype(F32), 0.0).astype(BF16)
        b_table = jnp.repeat(sgu_b[i].astype(F32).T, SGU_GROUP_WIDTH, axis=1)
        bmat, cmat, a_step, a_seg = _ssm_matrices(ssm_a_re[i], ssm_a_im[i], ssm_log_dt[i], ssm_b_re[i],
                                                  ssm_b_im[i], ssm_c_re[i], ssm_c_im[i])
        qkv, y_sgu, y_ssm = _mixin(
            h, row(norm_attn_g[i]), w_in[i].astype(BF16),
            (group_avg, row(sgu_ln_g[i]), row(sgu_ln_b[i]), w_causal, b_table),
            (bmat, cmat, a_step, a_seg, row(ssm_d[i]), ssm_glu_w[i].astype(BF16), row(ssm_glu_b[i])))

        y_attn = _attention(qkv, bias)

        h = _ffn(y_attn, y_sgu, y_ssm, h, p, i, row(branch_norm_g[i]), w_out[i].astype(BF16),
                 row(norm_ffn_g[i]), ffn_w_up[i].astype(BF16), ffn_conv_w[i].astype(F32),
                 row(ffn_conv_b[i]), ffn_w_down[i].astype(BF16), row(norm_ple_g[i]),
                 ple_w_gate[i].astype(BF16), ple_w_proj[i].astype(BF16), row(final_norm_g),
                 final=(i == DEPTH - 1))
    return h
```

```python
import functools
import math

import numpy as np
import jax
import jax.numpy as jnp
from jax import lax
from jax.experimental import pallas as pl
from jax.experimental.pallas import tpu as pltpu

D_MODEL = 1024
DEPTH = 2
PLE_DIM = 256
HEAD_DIM = 64
N_HEADS = 8
ATTN_WIDTH = N_HEADS * HEAD_DIM
BRANCHES = ((128, 1), (512, 4), (2048, 16))
ATTN_BLOCK = 128
N_REL_BUCKETS = 32
REL_MAX_DISTANCE = 2048
SGU_GROUPS = 4
SGU_GROUP_WIDTH = 64
SGU_WIDTH = SGU_GROUPS * SGU_GROUP_WIDTH
SGU_CHUNK = 128
SSM_GROUP_CH = 16
SSM_WIDTH = 256
SSM_GROUPS = SSM_WIDTH // SSM_GROUP_CH
SSM_STATE = 64
SSM_LANES = SSM_GROUPS * SSM_STATE
QKV_WIDTH = 3 * ATTN_WIDTH
IN_WIDTH = QKV_WIDTH + 2 * SGU_WIDTH + SSM_WIDTH
D_FF = 2816
EPS = 1e-6
NEG_INF = -1e30
LOG2E = 1.4426950408889634

LANES = 128
SUBLANES = 8
VMEM_LIMIT = 56 * 1024 * 1024

ROW_TILE = 512
ATTN_SUPER = ATTN_BLOCK * max(d for _, d in BRANCHES)
ATTN_MIX_ROWS = 256
ATTN_STAGE_DIL = 4
SSM_SEG = 64
SSM_TILE = SUBLANES * SSM_SEG
SSM_PITCH = SSM_SEG + SUBLANES
FF_CHUNK = 256
FF_GROUP = 6
GELU_C0 = 0.7978845608028654
GELU_C1 = GELU_C0 * 0.044715

BF16 = jnp.bfloat16
F32 = jnp.float32


def _gelu(x):
    return 0.5 * x * (1.0 + jnp.tanh(0.7978845608028654 * (x + 0.044715 * (x * x * x))))


def _sigmoid(x):
    return 1.0 / (1.0 + jnp.exp(-x))


def _rms_scale(x):
    return x * lax.rsqrt(jnp.mean(x * x, axis=-1, keepdims=True) + EPS)


def _dot(a, b):
    return jnp.dot(a, b, preferred_element_type=F32)


def _params(*sem):
    return pltpu.CompilerParams(dimension_semantics=sem, vmem_limit_bytes=VMEM_LIMIT)


def _resident(shape):
    nd = len(shape)
    return pl.BlockSpec(shape, lambda *_: (0,) * nd, pipeline_mode=pl.Buffered(1))


def _rows(start, size, stride):
    return pl.ds(start, size) if stride == 1 else pl.ds(start, size, stride=stride)


def _mixin_kernel(x_ref, g_ref, w_ref, avg_ref, lng_ref, lnb_ref, wsgu_ref, bt_ref, bmat_ref, cmat_ref, a_ref,
                  aseg_ref, d_ref, gw_ref, gb_ref, qkv_ref, ysgu_ref, yssm_ref, xn_scr, *ssm_scr):
    xn_scr[...] = (_rms_scale(x_ref[0]) * g_ref[...]).astype(BF16)
    o_s = QKV_WIDTH + 2 * SGU_WIDTH

    def qkv_part(c):
        def emit():
            qkv_ref[0, :, c:c + 512] = _dot(xn_scr[...], w_ref[:, c:c + 512])
        return emit

    def sgu_part():
        z_sgu = jnp.concatenate([_dot(xn_scr[...], w_ref[:, QKV_WIDTH + c:QKV_WIDTH + c + 256])
                                 for c in range(0, 2 * SGU_WIDTH, 256)], axis=1)
        _sgu_tile(z_sgu, avg_ref, lng_ref, lnb_ref, wsgu_ref, bt_ref, ysgu_ref)

    _ssm_tile(_dot(xn_scr[...], w_ref[:, o_s:]), pl.program_id(1) == 0, bmat_ref, cmat_ref, a_ref, aseg_ref,
              d_ref, gw_ref, gb_ref, yssm_ref, *ssm_scr,
              fillers=[qkv_part(c) for c in range(0, QKV_WIDTH, 512)] + [sgu_part])


def _mixin(h, g, w, sgu_args, ssm_args):
    B, S, _ = h.shape
    assert SSM_TILE == ROW_TILE and ROW_TILE % SGU_CHUNK == 0
    tile = lambda width: pl.BlockSpec((1, ROW_TILE, width), lambda b, s: (b, s, 0))
    return pl.pallas_call(
        _mixin_kernel,
        grid=(B, S // ROW_TILE),
        in_specs=[tile(D_MODEL), _resident((1, D_MODEL)), _resident((D_MODEL, IN_WIDTH))]
                 + [_resident(t.shape) for t in sgu_args] + [_resident(t.shape) for t in ssm_args],
        out_specs=[tile(QKV_WIDTH), tile(SGU_WIDTH), tile(SSM_WIDTH)],
        out_shape=[jax.ShapeDtypeStruct((B, S, QKV_WIDTH), F32),
                   jax.ShapeDtypeStruct((B, S, SGU_WIDTH), F32),
                   jax.ShapeDtypeStruct((B, S, SSM_WIDTH), F32)],
        scratch_shapes=[pltpu.VMEM((ROW_TILE, D_MODEL), BF16),
                        pltpu.VMEM((SSM_WIDTH // LANES, SUBLANES * SSM_PITCH, LANES), F32),
                        pltpu.VMEM((SSM_TILE, 2 * SSM_LANES), F32), pltpu.VMEM((2, SSM_LANES), F32),
                        pltpu.VMEM((SUBLANES, 2 * SSM_LANES), F32),
                        pltpu.VMEM((SSM_TILE, 2 * SSM_LANES), BF16)],
        compiler_params=_params("parallel", "arbitrary"),
        name="mixin",
    )(h, g, w, *sgu_args, *ssm_args)


def _t5_bucket(dist):
    max_exact = N_REL_BUCKETS // 2
    d = np.maximum(dist, 0)
    large = max_exact + (np.log(np.maximum(d, 1) / max_exact)
                         / np.log(REL_MAX_DISTANCE / max_exact)
                         * (N_REL_BUCKETS - max_exact)).astype(np.int32)
    large = np.minimum(large, N_REL_BUCKETS - 1)
    return np.where(d < max_exact, d, large).astype(np.int32)


def _bias_table(rel_bias, window, dil):
    blk = ATTN_BLOCK
    rel = np.arange(blk)[:, None] + blk - np.arange(2 * blk)[None, :]
    band = (rel >= 0) & (rel <= window // dil)
    bucket = np.where(band, _t5_bucket(rel * dil), -1)[None]
    table = jnp.full((N_HEADS, blk, 2 * blk), NEG_INF, F32)
    for n in range(N_REL_BUCKETS):
        if (bucket == n).any():
            table = jnp.where(bucket == n, (rel_bias[n].astype(F32) * LOG2E)[:, None, None], table)
    no_prev = jnp.where(np.arange(2 * blk)[None, None, :] < blk, NEG_INF, table)
    return jnp.stack([table, no_prev])


def _attn_kernel(q_ref, k_ref, v_ref, bias_ref, o_ref, *scr):
    nb = len(BRANCHES)
    k_scr, v_scr, o_scr, l_scr = scr[0:nb], scr[nb:2 * nb], scr[2 * nb:3 * nb], scr[3 * nb:4 * nb]
    blk = ATTN_BLOCK
    first_super = pl.program_id(2) == 0
    hp = pl.program_id(1)

    stage = scr[4 * nb:4 * nb + 2]
    sd = ATTN_STAGE_DIL
    for bi, (_, d) in enumerate(BRANCHES):
        n_d = ATTN_SUPER // d
        for ti, (src, dst) in enumerate(((k_ref, k_scr[bi]), (v_ref, v_scr[bi]))):
            @pl.when(first_super)
            def _(dst=dst):
                dst[:, 0:blk, :] = jnp.zeros((d, blk, LANES), BF16)

            @pl.when(jnp.logical_not(first_super))
            def _(dst=dst):
                dst[:, 0:blk, :] = dst[:, n_d:n_d + blk, :]

            for r in range(d):
                if d > sd and d % sd == 0:
                    x = stage[ti][r % sd, _rows(r // sd, n_d, d // sd), :]
                else:
                    x = src[_rows(r, n_d, d), :]
                    if d == sd:
                        stage[ti][r] = x
                dst[r, blk:, :] = x.astype(BF16)

    lane = lax.broadcasted_iota(jnp.int32, (blk, LANES), 1)
    low_half = lane < HEAD_DIM
    lane2 = lax.broadcasted_iota(jnp.int32, (2 * blk, LANES), 1)
    low2 = lane2 < HEAD_DIM
    zero_kv = jnp.zeros((2 * blk, LANES), BF16)
    ones_low = jnp.where(low2, 1.0, 0.0).astype(BF16)
    ones_high = jnp.where(low2, 0.0, 1.0).astype(BF16)

    units = []
    for bi, (_, d) in enumerate(BRANCHES):
        q_blocks = ATTN_SUPER // d // blk

        def unit(u, carry, bi=bi, d=d, q_blocks=q_blocks):
            r = u // q_blocks
            qb = u % q_blocks
            tok = _rows(r + qb * (blk * d), blk, d)
            variant = jnp.logical_and(first_super, qb == 0).astype(jnp.int32)
            q = q_ref[tok, :] * (HEAD_DIM ** -0.5 * LOG2E)
            q2 = jnp.concatenate([jnp.where(low_half, q, 0.0), jnp.where(low_half, 0.0, q)], axis=0).astype(BF16)
            win = pl.ds(pl.multiple_of(qb * blk, blk), 2 * blk)
            kk = k_scr[bi][r, win, :]
            vv = v_scr[bi][r, win, :]
            s = lax.dot_general(q2, kk, (((1,), (1,)), ((), ())), preferred_element_type=F32)
            s = s + jnp.concatenate([bias_ref[bi, variant, 2 * hp], bias_ref[bi, variant, 2 * hp + 1]], axis=0)
            m = jnp.max(s, axis=-1, keepdims=True)
            e = jnp.exp2(s - m).astype(BF16)
            e2 = jnp.concatenate([e[:blk], e[blk:]], axis=1)
            v2 = jnp.concatenate(
                [jnp.concatenate([jnp.where(low2, vv, zero_kv), ones_low], axis=1),
                 jnp.concatenate([jnp.where(low2, zero_kv, vv), ones_high], axis=1)], axis=0)
            oe = _dot(e2, v2)
            den = oe[:, LANES:]
            o_scr[bi][tok, :] = oe[:, :LANES] / den
            l_scr[bi][tok, :] = jnp.where(low_half, m[:blk], m[blk:]) + jnp.log2(den)
            return carry

        units.append(unit)

    def mix(c):
        rows = slice(c * ATTN_MIX_ROWS, (c + 1) * ATTN_MIX_ROWS)
        ls = [l[rows, :] for l in l_scr]
        m = functools.reduce(jnp.maximum, ls)
        es = [jnp.exp2(l - m) for l in ls]
        num = functools.reduce(lambda a, b: a + b, [e * o[rows, :] for e, o in zip(es, o_scr)])
        o_ref[rows, :] = (num / functools.reduce(lambda a, b: a + b, es)).astype(BF16)

    order = sorted(range(nb), key=lambda bi: -BRANCHES[bi][1])
    assert BRANCHES[order[-1]][1] == 1
    for bi in order[:-1]:
        lax.fori_loop(0, ATTN_SUPER // blk, units[bi], 0, unroll=ATTN_SUPER // blk)
    per_mix = ATTN_MIX_ROWS // blk
    for u in range(ATTN_SUPER // blk):
        units[order[-1]](jnp.int32(u), 0)
        if (u + 1) % per_mix == 0:
            mix(u // per_mix)


def _attention(qkv, bias):
    B, S, _ = qkv.shape
    pairs = ATTN_WIDTH // LANES
    slab = lambda part: pl.BlockSpec((None, ATTN_SUPER, LANES), lambda b, hp, s: (b, s, part * pairs + hp))
    kv_scr = [pltpu.VMEM((d, ATTN_BLOCK + ATTN_SUPER // d, LANES), BF16) for _, d in BRANCHES]
    tok_scr = [pltpu.VMEM((ATTN_SUPER, LANES), F32) for _ in BRANCHES]
    stage_scr = pltpu.VMEM((ATTN_STAGE_DIL, ATTN_SUPER // ATTN_STAGE_DIL, LANES), F32)
    return pl.pallas_call(
        _attn_kernel,
        grid=(B, pairs, S // ATTN_SUPER),
        in_specs=[slab(0), slab(1), slab(2), _resident(bias.shape)],
        out_specs=pl.BlockSpec((None, ATTN_SUPER, LANES), lambda b, hp, s: (b, s, hp)),
        out_shape=jax.ShapeDtypeStruct((B, S, ATTN_WIDTH), BF16),
        scratch_shapes=kv_scr + kv_scr + tok_scr + tok_scr + [stage_scr, stage_scr],
        compiler_params=_params("parallel", "parallel", "arbitrary"),
        name="attn",
    )(qkv, qkv, qkv, bias)


def _sgu_tile(z, avg_ref, lng_ref, lnb_ref, w_ref, bt_ref, y_ref):
    z = _gelu(z)
    u = z[:, :SGU_WIDTH]
    v = z[:, SGU_WIDTH:]
    avg = avg_ref[...]

    def group_mean(t):
        hi = t.astype(BF16)
        lo = (t - hi.astype(F32)).astype(BF16)
        return _dot(hi, avg) + _dot(lo, avg)

    d = v - group_mean(v)
    vn = d * lax.rsqrt(group_mean(d * d) + EPS) * lng_ref[...] + lnb_ref[...]
    lane = lax.broadcasted_iota(jnp.int32, (SGU_CHUNK, LANES), 1)
    low_half = lane < SGU_GROUP_WIDTH
    for c in range(ROW_TILE // SGU_CHUNK):
        rows = slice(c * SGU_CHUNK, (c + 1) * SGU_CHUNK)
        for p in range(SGU_GROUPS // 2):
            cols = slice(p * LANES, (p + 1) * LANES)
            vp = vn[rows, cols].astype(BF16)
            mixed = jnp.where(low_half, _dot(w_ref[2 * p], vp), _dot(w_ref[2 * p + 1], vp))
            y_ref[0, rows, cols] = u[rows, cols] * (mixed + bt_ref[:, cols])


def _ssm_tile(z, first_tile, bmat_ref, cmat_ref, a_ref, aseg_ref, d_ref, gw_ref, gb_ref, y_ref,
              perm_scr, x_scr, carry_scr, xin_scr, xb_scr, fillers):
    N = SSM_LANES
    slabs = SSM_WIDTH // LANES

    @pl.when(first_tile)
    def _():
        carry_scr[...] = jnp.zeros_like(carry_scr)

    fillers[0]()
    for j in range(SUBLANES):
        for c in range(slabs):
            perm_scr[c, j * SSM_PITCH:j * SSM_PITCH + SSM_SEG, :] = \
                z[j * SSM_SEG:(j + 1) * SSM_SEG, c * LANES:(c + 1) * LANES]
    u = jnp.concatenate(
        [jnp.concatenate([perm_scr[c, pl.ds(t, SUBLANES, stride=SSM_PITCH), :] for c in range(slabs)], axis=1)
         for t in range(SSM_SEG)], axis=0)
    x_scr[...] = _dot(u.astype(BF16), bmat_ref[...])
    fillers[1]()

    a_re = jnp.broadcast_to(a_ref[0:1, :], (SUBLANES, N))
    a_im = jnp.broadcast_to(a_ref[1:2, :], (SUBLANES, N))

    end_re = end_im = jnp.zeros((SUBLANES, N), F32)
    for t in range(SSM_SEG):
        rows = slice(t * SUBLANES, (t + 1) * SUBLANES)
        end_re, end_im = (a_re * end_re - a_im * end_im + x_scr[rows, 0:N],
                          a_re * end_im + a_im * end_re + x_scr[rows, N:2 * N])
        x_scr[rows, 0:N] = end_re
        x_scr[rows, N:2 * N] = end_im

    s_re, s_im = aseg_ref[0:1, :], aseg_ref[1:2, :]
    cr, ci = carry_scr[0:1, :], carry_scr[1:2, :]
    for j in range(SUBLANES):
        xin_scr[j:j + 1, 0:N] = cr
        xin_scr[j:j + 1, N:2 * N] = ci
        er, ei = end_re[j:j + 1, :], end_im[j:j + 1, :]
        cr, ci = s_re * cr - s_im * ci + er, s_re * ci + s_im * cr + ei
    carry_scr[0:1, :] = cr
    carry_scr[1:2, :] = ci

    fillers[2]()
    fr, fi = xin_scr[:, 0:N], xin_scr[:, N:2 * N]
    for i in range(SSM_SEG // 2):
        out_re, out_im = [], []
        for k in range(2):
            fr, fi = a_re * fr - a_im * fi, a_re * fi + a_im * fr
            rows = slice((2 * i + k) * SUBLANES, (2 * i + k + 1) * SUBLANES)
            out_re.append(x_scr[rows, 0:N] + fr)
            out_im.append(x_scr[rows, N:2 * N] + fi)
        rows2 = slice(2 * i * SUBLANES, (2 * i + 2) * SUBLANES)
        xb_scr[rows2, 0:N] = jnp.concatenate(out_re, axis=0).astype(BF16)
        xb_scr[rows2, N:2 * N] = jnp.concatenate(out_im, axis=0).astype(BF16)

    fillers[3]()
    y = _dot(xb_scr[...], cmat_ref[...]) + d_ref[...] * u
    y = _gelu(y)
    y = y * _sigmoid(_dot(y.astype(BF16), gw_ref[...]) + gb_ref[...])
    for t in range(SSM_SEG):
        for c in range(slabs):
            perm_scr[c, pl.ds(t, SUBLANES, stride=SSM_PITCH), :] = \
                y[t * SUBLANES:(t + 1) * SUBLANES, c * LANES:(c + 1) * LANES]
    for j in range(SUBLANES):
        for c in range(slabs):
            y_ref[0, j * SSM_SEG:(j + 1) * SSM_SEG, c * LANES:(c + 1) * LANES] = \
                perm_scr[c, j * SSM_PITCH:j * SSM_PITCH + SSM_SEG, :]


def _ssm_matrices(a_re, a_im, log_dt, b_re, b_im, c_re, c_im):
    G, N, C = SSM_GROUPS, SSM_STATE, SSM_GROUP_CH
    dt = jnp.exp(log_dt)[:, None]
    mag = jnp.exp(a_re * dt)
    ab_re, ab_im = mag * jnp.cos(a_im * dt), mag * jnp.sin(a_im * dt)
    den = a_re * a_re + a_im * a_im
    f_re = ((ab_re - 1.0) * a_re + ab_im * a_im) / den
    f_im = (ab_im * a_re - (ab_re - 1.0) * a_im) / den
    bb_re = f_re[:, :, None] * b_re - f_im[:, :, None] * b_im
    bb_im = f_re[:, :, None] * b_im + f_im[:, :, None] * b_re
    eye = jnp.eye(G, dtype=F32)
    bd_in = lambda t: jnp.einsum('gnc,gh->gchn', t, eye).reshape(G * C, G * N)
    bd_out = lambda t: jnp.einsum('gcn,gh->gnhc', t, eye).reshape(G * N, G * C)
    bmat = jnp.concatenate([bd_in(bb_re), bd_in(bb_im)], axis=1).astype(BF16)
    cmat = jnp.concatenate([bd_out(c_re), -bd_out(c_im)], axis=0).astype(BF16)
    a_step = jnp.stack([ab_re.reshape(-1), ab_im.reshape(-1)])
    pr, pi = ab_re, ab_im
    for _ in range(int(math.log2(SSM_SEG))):
        pr, pi = pr * pr - pi * pi, 2.0 * pr * pi
    a_seg = jnp.stack([pr.reshape(-1), pi.reshape(-1)])
    return bmat, cmat, a_step, a_seg


def _ffn_kernel(attn_ref, sgu_ref, ssm_ref, h_ref, p_ref, gmix_ref, wout_ref, gffn_ref, wup_ref, cw_ref, cb_ref,
                wdown_ref, gple_ref, wgate_ref, wproj_ref, gfin_ref, out_ref, xn_scr, tail_scr, shift_scr, act_scr,
                *, final):
    n_chunks = D_FF // FF_CHUNK
    first_tile = pl.program_id(1) == 0

    h = h_ref[0]
    lo = 0
    for part in (attn_ref[0].astype(F32), sgu_ref[0], ssm_ref[0]):
        hi = lo + part.shape[-1]
        h = h + _dot((_rms_scale(part) * gmix_ref[:, lo:hi]).astype(BF16), wout_ref[lo:hi, :])
        lo = hi

    xn_scr[...] = (_rms_scale(h) * gffn_ref[...]).astype(BF16)
    out_ref[0] = h

    def conv(t, col0, slot, buf):
        cols = slice(col0, col0 + FF_CHUNK)
        buf[0:SUBLANES, :] = jnp.where(first_tile, 0.0, tail_scr[slot])
        buf[SUBLANES:, :] = t
        tail_scr[slot] = t[ROW_TILE - SUBLANES:]
        m1 = buf[SUBLANES - 1:SUBLANES - 1 + ROW_TILE, :]
        m2 = buf[SUBLANES - 2:SUBLANES - 2 + ROW_TILE, :]
        return cw_ref[2:3, cols] * t + cw_ref[1:2, cols] * m1 + cw_ref[0:1, cols] * m2 + cb_ref[:, cols]

    def down(g, width, act):
        out_ref[0] += _dot(act[:, :width], wdown_ref[g * FF_CHUNK:g * FF_CHUNK + width, :])

    pending = None
    for gi, g in enumerate(range(0, n_chunks, FF_GROUP)):
        group = range(g, min(g + FF_GROUP, n_chunks))
        act = act_scr.at[gi % 2]
        for c in group:
            v0, g0 = c * FF_CHUNK, D_FF + c * FF_CHUNK
            xn = xn_scr[...]
            val = conv(_dot(xn, wup_ref[:, v0:v0 + FF_CHUNK]), v0, 2 * c, shift_scr.at[2 * (c % 2)])
            gate = conv(_dot(xn, wup_ref[:, g0:g0 + FF_CHUNK]), g0, 2 * c + 1, shift_scr.at[2 * (c % 2) + 1])
            tanh = jnp.tanh(gate * (GELU_C0 + GELU_C1 * (gate * gate)))
            act[:, (c - g) * FF_CHUNK:(c - g + 1) * FF_CHUNK] = ((gate * val) * (0.5 + 0.5 * tanh)).astype(BF16)
            if pending is not None and c == g:
                down(*pending)
                pending = None
        pending = (g, len(group) * FF_CHUNK, act)
    down(*pending)

    h2 = out_ref[0]
    gate = _sigmoid(_dot((_rms_scale(h2) * gple_ref[...]).astype(BF16), wgate_ref[...]))
    h3 = h2 + gate * _dot(p_ref[...].astype(BF16), wproj_ref[...])
    if final:
        h3 = _rms_scale(h3) * gfin_ref[...]
    out_ref[0] = h3


def _ffn(y_attn, y_sgu, y_ssm, h, p, layer, g_mix, w_out, g_ffn, w_up, conv_w, conv_b, w_down, g_ple, w_gate,
         w_proj, g_final, final):
    B, S, _ = h.shape
    tile = lambda width: pl.BlockSpec((1, ROW_TILE, width), lambda b, s: (b, s, 0))
    return pl.pallas_call(
        functools.partial(_ffn_kernel, final=final),
        grid=(B, S // ROW_TILE),
        in_specs=[tile(ATTN_WIDTH), tile(SGU_WIDTH), tile(SSM_WIDTH), tile(D_MODEL),
                  pl.BlockSpec((None, None, ROW_TILE, PLE_DIM), lambda b, s: (layer, b, s, 0)),
                  _resident((1, D_MODEL)), _resident((D_MODEL, D_MODEL)),
                  _resident((1, D_MODEL)), _resident((D_MODEL, 2 * D_FF)),
                  _resident((3, 2 * D_FF)), _resident((1, 2 * D_FF)), _resident((D_FF, D_MODEL)),
                  _resident((1, D_MODEL)), _resident((D_MODEL, D_MODEL)), _resident((PLE_DIM, D_MODEL)),
                  _resident((1, D_MODEL))],
        out_specs=tile(D_MODEL),
        out_shape=jax.ShapeDtypeStruct(h.shape, F32),
        scratch_shapes=[pltpu.VMEM((ROW_TILE, D_MODEL), BF16),
                        pltpu.VMEM((2 * (D_FF // FF_CHUNK), SUBLANES, FF_CHUNK), F32),
                        pltpu.VMEM((4, SUBLANES + ROW_TILE, FF_CHUNK), F32),
                        pltpu.VMEM((2, ROW_TILE, FF_GROUP * FF_CHUNK), BF16)],
        compiler_params=_params("parallel", "arbitrary"),
        name="ffn",
    )(y_attn, y_sgu, y_ssm, h, p, g_mix, w_out, g_ffn, w_up, conv_w, conv_b, w_down, g_ple, w_gate, w_proj, g_final)


def kernel(x, p, rel_bias, norm_attn_g, w_in, sgu_ln_g, sgu_ln_b, sgu_w, sgu_b, ssm_a_re, ssm_a_im, ssm_log_dt, ssm_b_re, ssm_b_im, ssm_c_re, ssm_c_im, ssm_d, ssm_glu_w, ssm_glu_b, branch_norm_g, w_out, norm_ffn_g, ffn_w_up, ffn_conv_w, ffn_conv_b, ffn_w_down, norm_ple_g, ple_w_gate, ple_w_proj, final_norm_g):
    row = lambda t: t.reshape(1, -1).astype(F32)
    bias = jnp.stack([_bias_table(rel_bias, window, dil) for window, dil in BRANCHES])
    group_avg = jnp.asarray(np.kron(np.eye(SGU_GROUPS), np.full((SGU_GROUP_WIDTH,) * 2, 1.0 / SGU_GROUP_WIDTH)), BF16)
    causal = np.tril(np.ones((SGU_CHUNK, SGU_CHUNK), dtype=bool))

    h = x
    for i in range(DEPTH):
        w_causal = jnp.where(causal, sgu_w[i].astype(F32), 0.0).astype(BF16)
        b_table = jnp.repeat(sgu_b[i].astype(F32).T, SGU_GROUP_WIDTH, axis=1)
        bmat, cmat, a_step, a_seg = _ssm_matrices(ssm_a_re[i], ssm_a_im[i], ssm_log_dt[i], ssm_b_re[i],
                                                  ssm_b_im[i], ssm_c_re[i], ssm_c_im[i])
        qkv, y_sgu, y_ssm = _mixin(
            h, row(norm_attn_g[i]), w_in[i].astype(BF16),
            (group_avg, row(sgu_ln_g[i]), row(sgu_ln_b[i]), w_causal, b_table),
            (bmat, cmat, a_step, a_seg, row(ssm_d[i]), ssm_glu_w[i].astype(BF16), row(ssm_glu_b[i])))

        y_attn = _attention(qkv, bias)

        h = _ffn(y_attn, y_sgu, y_ssm, h, p, i, row(branch_norm_g[i]), w_out[i].astype(BF16),
                 row(norm_ffn_g[i]), ffn_w_up[i].astype(BF16), ffn_conv_w[i].astype(F32),
                 row(ffn_conv_b[i]), ffn_w_down[i].astype(BF16), row(norm_ple_g[i]),
                 ple_w_gate[i].astype(BF16), ple_w_proj[i].astype(BF16), row(final_norm_g),
                 final=(i == DEPTH - 1))
    return h
```
